```python
import jax
import jax.numpy as jnp
from jax import lax
import numpy as np

D_MODEL = 4096
BATCH = 4
SEQ = 2048
DEPTH = 2

NORM_EPS = 1e-6
QBLOCK = 128

FOX_HEADS = 8
FOX_HEAD_DIM = 128
FOX_WIDTH = FOX_HEADS * FOX_HEAD_DIM

MLSTM_HEADS = 8
MLSTM_QK_DIM = 64
MLSTM_V_DIM = 128
MLSTM_QK_WIDTH = MLSTM_HEADS * MLSTM_QK_DIM
MLSTM_V_WIDTH = MLSTM_HEADS * MLSTM_V_DIM
MLSTM_CONV_WIDTH = 4
MLSTM_CHUNK = 64

MLA_HEADS = 8
MLA_Q_RANK = 896
MLA_KV_RANK = 512
MLA_NOPE_DIM = 128
MLA_ROPE_DIM = 64
MLA_V_DIM = 128
MLA_QK_DIM = MLA_NOPE_DIM + MLA_ROPE_DIM
MLA_WIDTH = MLA_HEADS * MLA_V_DIM
ROPE_THETA = 10000.0

N_BRANCH = 3
D_FF = -(-8 * D_MODEL // (3 * 256)) * 256

IN_SIZES = (
    FOX_WIDTH, FOX_WIDTH, FOX_WIDTH, FOX_HEADS,
    MLSTM_QK_WIDTH, MLSTM_QK_WIDTH, MLSTM_V_WIDTH,
    MLSTM_HEADS, MLSTM_HEADS, MLSTM_V_WIDTH,
    MLA_Q_RANK, MLA_KV_RANK, MLA_ROPE_DIM,
    N_BRANCH * D_MODEL,
)
D_IN = sum(IN_SIZES)

kernel_name = 'hybrid_fox_mlstm_mla_gated'


def _rmsnorm(x, gain):
    xf = x.astype(jnp.float32)
    y = xf * lax.rsqrt(jnp.mean(xf * xf, axis=-1, keepdims=True) + NORM_EPS)
    return (y * gain.astype(jnp.float32)).astype(x.dtype)


def _split_heads(x, n_heads):
    b, s, _ = x.shape
    return x.reshape(b, s, n_heads, -1).transpose(0, 2, 1, 3)


def _merge_heads(x):
    b, h, s, d = x.shape
    return x.transpose(0, 2, 1, 3).reshape(b, s, h * d)


def _rope_tables(positions):
    inv_freq = jnp.power(ROPE_THETA, -jnp.arange(0, MLA_ROPE_DIM, 2, dtype=jnp.float32) / MLA_ROPE_DIM)
    ang = positions.astype(jnp.float32)[:, None, :, None] * inv_freq
    return jnp.cos(ang), jnp.sin(ang)


def _apply_rope(x, cos, sin):
    xf = x.astype(jnp.float32)
    x1, x2 = jnp.split(xf, 2, axis=-1)
    return jnp.concatenate([x1 * cos - x2 * sin, x2 * cos + x1 * sin], axis=-1).astype(x.dtype)


def _block_causal_attention(q, k, v, scale, cum_log_f=None):
    b, h, s, dk = q.shape
    nb = s // QBLOCK
    q_blocks = jnp.moveaxis(q.reshape(b, h, nb, QBLOCK, dk), 2, 0)
    starts = jnp.arange(nb) * QBLOCK
    key_pos = jnp.arange(s)

    def one_block(args):
        qb, start = args
        logits = jnp.einsum('bhqd,bhkd->bhqk', qb, k, preferred_element_type=jnp.float32) * scale
        if cum_log_f is not None:
            d_q = lax.dynamic_slice_in_dim(cum_log_f, start, QBLOCK, axis=2)
            logits = logits + d_q[..., :, None] - cum_log_f[..., None, :]
        q_pos = start + jnp.arange(QBLOCK)
        logits = jnp.where(key_pos[None, :] <= q_pos[:, None], logits, -jnp.inf)
        p = jax.nn.softmax(logits, axis=-1)
        return jnp.einsum('bhqk,bhkd->bhqd', p.astype(v.dtype), v)

    out = lax.map(one_block, (q_blocks, starts))
    return jnp.moveaxis(out, 0, 2).reshape(b, h, s, v.shape[-1])


def _fox_mixer(q_raw, k_raw, v_raw, f_pre, f_bias, q_gain, k_gain):
    q = _rmsnorm(_split_heads(q_raw, FOX_HEADS), q_gain)
    k = _rmsnorm(_split_heads(k_raw, FOX_HEADS), k_gain)
    v = _split_heads(v_raw, FOX_HEADS)
    log_f = jax.nn.log_sigmoid(f_pre.astype(jnp.float32) + f_bias.astype(jnp.float32))
    cum_log_f = jnp.cumsum(jnp.swapaxes(log_f, 1, 2), axis=-1)
    o = _block_causal_attention(q, k, v, FOX_HEAD_DIM ** -0.5, cum_log_f)
    return _merge_heads(o)


def _causal_depthwise_conv(x, w, bias):
    y = lax.conv_general_dilated(
        x, w[:, None, :].astype(x.dtype), window_strides=(1,),
        padding=[(MLSTM_CONV_WIDTH - 1, 0)],
        dimension_numbers=('NWC', 'WIO', 'NWC'),
        feature_group_count=x.shape[-1])
    return y + bias.astype(x.dtype)


def _mlstm_chunkwise(q, k, v, log_i, log_f):
    b, h, s, dk = q.shape
    dv = v.shape[-1]
    nc = s // MLSTM_CHUNK

    def chunks(a):
        return jnp.moveaxis(a.reshape(b, h, nc, MLSTM_CHUNK, *a.shape[3:]), 2, 0)

    causal = jnp.tril(jnp.ones((MLSTM_CHUNK, MLSTM_CHUNK), dtype=bool))

    def step(carry, inp):
        c_state, n_state, m_prev = carry
        qt, kt, vt, li, lf = inp
        bcum = jnp.cumsum(lf, axis=-1)
        log_w = bcum[..., :, None] - bcum[..., None, :] + li[..., None, :]
        log_w = jnp.where(causal, log_w, -jnp.inf)
        log_inter = bcum + m_prev[..., None]
        m_t = jnp.maximum(log_inter, jnp.max(log_w, axis=-1))
        w = jnp.exp(log_w - m_t[..., None])
        inter = jnp.exp(log_inter - m_t)
        qk = jnp.einsum('bhtd,bhsd->bhts', qt, kt) * w
        num = (jnp.einsum('bhts,bhse->bhte', qk, vt)
               + inter[..., None] * jnp.einsum('bhtd,bhde->bhte', qt, c_state))
        den = jnp.sum(qk, axis=-1) + inter * jnp.einsum('bhtd,bhd->bht', qt, n_state)
        h_out = num / jnp.maximum(jnp.abs(den), jnp.exp(-m_t))[..., None]
        m_new = m_t[..., -1]
        state_decay = jnp.exp(log_inter[..., -1] - m_new)
        w_last = jnp.exp(bcum[..., -1:] - bcum + li - m_new[..., None])
        c_new = (state_decay[..., None, None] * c_state
                 + jnp.einsum('bhs,bhsd,bhse->bhde', w_last, kt, vt))
        n_new = state_decay[..., None] * n_state + jnp.einsum('bhs,bhsd->bhd', w_last, kt)
        return (c_new, n_new, m_new), h_out

    init = (jnp.zeros((b, h, dk, dv), jnp.float32),
            jnp.zeros((b, h, dk), jnp.float32),
            jnp.zeros((b, h), jnp.float32))
    _, hs = lax.scan(step, init, (chunks(q), chunks(k), chunks(v), chunks(log_i), chunks(log_f)))
    return jnp.moveaxis(hs, 0, 2).reshape(b, h, s, dv)


def _mlstm_mixer(q_pre, k_pre, v_raw, i_pre, f_pre, o_pre, conv_w, conv_b, i_bias, f_bias, out_gain):
    qk = jax.nn.silu(_causal_depthwise_conv(jnp.concatenate([q_pre, k_pre], axis=-1), conv_w, conv_b))
    q_c, k_c = jnp.split(qk, 2, axis=-1)
    q = _split_heads(q_c, MLSTM_HEADS).astype(jnp.float32)
    k = _split_heads(k_c, MLSTM_HEADS).astype(jnp.float32) * (MLSTM_QK_DIM ** -0.5)
    v = _split_heads(v_raw, MLSTM_HEADS).astype(jnp.float32)
    log_i = jnp.swapaxes(i_pre.astype(jnp.float32) + i_bias.astype(jnp.float32), 1, 2)
    log_f = jnp.swapaxes(jax.nn.log_sigmoid(f_pre.astype(jnp.float32) + f_bias.astype(jnp.float32)), 1, 2)
    h = _mlstm_chunkwise(q, k, v, log_i, log_f)
    b, _, s, _ = h.shape
    h = _rmsnorm(h.transpose(0, 2, 1, 3), out_gain).reshape(b, s, MLSTM_V_WIDTH)
    return (h * jax.nn.sigmoid(o_pre.astype(jnp.float32))).astype(v_raw.dtype)


def _mla_mixer(c_q, c_kv, k_rope_raw, cq_gain, ckv_gain, w_uq, w_ukv, q_gain, k_gain, cos, sin):
    b, s, _ = c_q.shape
    q = _split_heads(_rmsnorm(c_q, cq_gain) @ w_uq, MLA_HEADS)
    kv = (_rmsnorm(c_kv, ckv_gain) @ w_ukv).reshape(b, s, MLA_HEADS, MLA_NOPE_DIM + MLA_V_DIM)
    k_nope, v = jnp.split(kv, [MLA_NOPE_DIM], axis=-1)
    k_rope = jnp.broadcast_to(k_rope_raw[:, :, None, :], (b, s, MLA_HEADS, MLA_ROPE_DIM))
    k = jnp.concatenate([k_nope, k_rope], axis=-1).transpose(0, 2, 1, 3)
    v = v.transpose(0, 2, 1, 3)
    q = _rmsnorm(q, q_gain)
    k = _rmsnorm(k, k_gain)
    q = jnp.concatenate([q[..., :MLA_NOPE_DIM], _apply_rope(q[..., MLA_NOPE_DIM:], cos, sin)], axis=-1)
    k = jnp.concatenate([k[..., :MLA_NOPE_DIM], _apply_rope(k[..., MLA_NOPE_DIM:], cos, sin)], axis=-1)
    o = _block_causal_attention(q, k, v, MLA_QK_DIM ** -0.5)
    return _merge_heads(o)


def _hybrid_layer(x, cos, sin, mix_norm, w_in, fox_f_bias, fox_q_gain, fox_k_gain,
                  mlstm_conv_w, mlstm_conv_b, mlstm_i_bias, mlstm_f_bias, mlstm_out_gain,
                  mla_cq_gain, mla_ckv_gain, mla_w_uq, mla_w_ukv, mla_q_gain, mla_k_gain,
                  w_fox_out, w_mlstm_out, w_mla_out, gate_bias, w_o,
                  ffn_norm, w_gate, w_up, w_down):
    b, s, d = x.shape
    h = _rmsnorm(x, mix_norm)
    proj = h @ w_in
    (fq, fk, fv, ff, mq, mk, mv, mi, mf, mo, cq, ckv, kr, g_pre) = jnp.split(
        proj, np.cumsum(IN_SIZES)[:-1].tolist(), axis=-1)
    y_fox = _fox_mixer(fq, fk, fv, ff, fox_f_bias, fox_q_gain, fox_k_gain) @ w_fox_out
    y_mlstm = _mlstm_mixer(mq, mk, mv, mi, mf, mo, mlstm_conv_w, mlstm_conv_b,
                           mlstm_i_bias, mlstm_f_bias, mlstm_out_gain) @ w_mlstm_out
    y_mla = _mla_mixer(cq, ckv, kr, mla_cq_gain, mla_ckv_gain, mla_w_uq, mla_w_ukv,
                       mla_q_gain, mla_k_gain, cos, sin) @ w_mla_out
    gates = jax.nn.sigmoid(g_pre.astype(jnp.float32) + gate_bias.astype(jnp.float32))
    gates = gates.astype(x.dtype).reshape(b, s, N_BRANCH, d)
    merged = gates[:, :, 0] * y_fox + gates[:, :, 1] * y_mlstm + gates[:, :, 2] * y_mla
    x = x + merged @ w_o
    h = _rmsnorm(x, ffn_norm)
    return x + (jax.nn.silu(h @ w_gate) * (h @ w_up)) @ w_down


def setup_inputs(seed: int = 0) -> dict:
    key = jax.random.key(seed)
    ks = iter(jax.random.split(key, 40))

    def nrm(shape, scale):
        return scale * jax.random.normal(next(ks), shape, jnp.float32)

    def gain(shape):
        return 1.0 + nrm(shape, 0.02)

    L = DEPTH
    x = jax.random.normal(next(ks), (BATCH, SEQ, D_MODEL), jnp.float32)
    positions = (jnp.arange(SEQ, dtype=jnp.int32)[None, :]
                 + jax.random.randint(next(ks), (BATCH, 1), 0, 1024, dtype=jnp.int32))
    return {
        'x': x,
        'positions': positions,
        'mix_norm': gain((L, D_MODEL)),
        'w_in': nrm((L, D_MODEL, D_IN), D_MODEL ** -0.5),
        'fox_f_bias': jnp.linspace(1.0, 5.0, FOX_HEADS, dtype=jnp.float32)[None, :] + nrm((L, FOX_HEADS), 0.1),
        'fox_q_gain': gain((L, FOX_HEAD_DIM)),
        'fox_k_gain': gain((L, FOX_HEAD_DIM)),
        'mlstm_conv_w': nrm((L, MLSTM_CONV_WIDTH, 2 * MLSTM_QK_WIDTH), MLSTM_CONV_WIDTH ** -0.5),
        'mlstm_conv_b': nrm((L, 2 * MLSTM_QK_WIDTH), 0.02),
        'mlstm_i_bias': nrm((L, MLSTM_HEADS), 0.1),
        'mlstm_f_bias': jnp.linspace(3.0, 6.0, MLSTM_HEADS, dtype=jnp.float32)[None, :] + nrm((L, MLSTM_HEADS), 0.1),
        'mlstm_out_gain': gain((L, MLSTM_HEADS, MLSTM_V_DIM)),
        'mla_cq_gain': gain((L, MLA_Q_RANK)),
        'mla_ckv_gain': gain((L, MLA_KV_RANK)),
        'mla_w_uq': nrm((L, MLA_Q_RANK, MLA_HEADS * MLA_QK_DIM), MLA_Q_RANK ** -0.5),
        'mla_w_ukv': nrm((L, MLA_KV_RANK, MLA_HEADS * (MLA_NOPE_DIM + MLA_V_DIM)), MLA_KV_RANK ** -0.5),
        'mla_q_gain': gain((L, MLA_QK_DIM)),
        'mla_k_gain': gain((L, MLA_QK_DIM)),
        'w_fox_out': nrm((L, FOX_WIDTH, D_MODEL), FOX_WIDTH ** -0.5),
        'w_mlstm_out': nrm((L, MLSTM_V_WIDTH, D_MODEL), MLSTM_V_WIDTH ** -0.5),
        'w_mla_out': nrm((L, MLA_WIDTH, D_MODEL), MLA_WIDTH ** -0.5),
        'gate_bias': nrm((L, N_BRANCH * D_MODEL), 0.01),
        'w_o': nrm((L, D_MODEL, D_MODEL), D_MODEL ** -0.5),
        'ffn_norm': gain((L, D_MODEL)),
        'w_gate': nrm((L, D_MODEL, D_FF), D_MODEL ** -0.5),
        'w_up': nrm((L, D_MODEL, D_FF), D_MODEL ** -0.5),
        'w_down': nrm((L, D_FF, D_MODEL), D_FF ** -0.5),
    }


def reference(x, positions, mix_norm, w_in, fox_f_bias, fox_q_gain, fox_k_gain,
              mlstm_conv_w, mlstm_conv_b, mlstm_i_bias, mlstm_f_bias, mlstm_out_gain,
              mla_cq_gain, mla_ckv_gain, mla_w_uq, mla_w_ukv, mla_q_gain, mla_k_gain,
              w_fox_out, w_mlstm_out, w_mla_out, gate_bias, w_o,
              ffn_norm, w_gate, w_up, w_down):
    cos, sin = _rope_tables(positions)
    for l in range(DEPTH):
        x = _hybrid_layer(
            x, cos, sin, mix_norm[l], w_in[l], fox_f_bias[l], fox_q_gain[l], fox_k_gain[l],
            mlstm_conv_w[l], mlstm_conv_b[l], mlstm_i_bias[l], mlstm_f_bias[l], mlstm_out_gain[l],
            mla_cq_gain[l], mla_ckv_gain[l], mla_w_uq[l], mla_w_ukv[l], mla_q_gain[l], mla_k_gain[l],
            w_fox_out[l], w_mlstm_out[l], w_mla_out[l], gate_bias[l], w_o[l],
            ffn_norm[l], w_gate[l], w_up[l], w_down[l])
    return x
```

```python
import functools

import jax
import jax.numpy as jnp
from jax import lax
from jax.experimental import pallas as pl
from jax.experimental.pallas import tpu as pltpu

F32 = jnp.float32
BF16 = jnp.bfloat16

NORM_EPS = 1e-6
ROPE_THETA = 10000.0

FOX_HEADS = 8
FOX_DIM = 128
ML_HEADS = 8
ML_QK = 64
ML_V = 128
ML_CONV = 4
MLA_HEADS = 8
MLA_Q_RANK = 896
MLA_KV_RANK = 512
MLA_NOPE = 128
MLA_ROPE = 64
MLA_V = 128
MLA_QK = MLA_NOPE + MLA_ROPE
MLA_PAD = 256

LANES = 128
VMEM_LIMIT = 56 * 1024 * 1024
ML_CHUNK = 256
ATT_BLOCK = 256
HALO = 16


def _params(*sem):
    return pltpu.CompilerParams(dimension_semantics=sem, vmem_limit_bytes=VMEM_LIMIT)


def _rmsnorm_kernel(x_ref, g_ref, o_ref):
    x = x_ref[...]
    ms = jnp.mean(x * x, axis=-1, keepdims=True)
    o_ref[...] = (x * lax.rsqrt(ms + NORM_EPS) * g_ref[...]).astype(o_ref.dtype)


def _rmsnorm(x, gain, tm=256):
    m, d = x.shape
    return pl.pallas_call(
        _rmsnorm_kernel,
        grid=(m // tm,),
        in_specs=[pl.BlockSpec((tm, d), lambda i: (i, 0)),
                  pl.BlockSpec((1, d), lambda i: (0, 0))],
        out_specs=pl.BlockSpec((tm, d), lambda i: (i, 0)),
        out_shape=jax.ShapeDtypeStruct((m, d), BF16),
        compiler_params=_params("parallel"),
        name="rmsnorm",
    )(x, gain.reshape(1, d))


def _ws_kernel(*refs, nx, nw, ne, pairs, cast, epilogue, cast_rows):
    x_refs = refs[:nx]
    w_refs = refs[nx:nx + nw]
    e_refs = refs[nx + nw:nx + nw + ne]
    o_ref = refs[nx + nw + ne]
    s_refs = refs[nx + nw + ne + 1:]
    if cast:
        @pl.when(pl.program_id(1) == 0)
        def _():
            for w_ref, s_ref in zip(w_refs, s_refs):
                k = w_ref.shape[0]

                def body(i, c, w_ref=w_ref, s_ref=s_ref):
                    r = pl.multiple_of(i * cast_rows, cast_rows)
                    s_ref[pl.ds(r, cast_rows), :] = w_ref[pl.ds(r, cast_rows), :].astype(BF16)
                    return c

                lax.fori_loop(0, k // cast_rows, body, 0)
        w_src = s_refs
    else:
        w_src = w_refs
    dots = [jnp.dot(x_refs[xi][...], w_src[wi][...], preferred_element_type=F32) for xi, wi in pairs]
    o_ref[...] = epilogue(dots, [e[...] for e in e_refs]).astype(o_ref.dtype)


def _ws_matmul(xs, ws, extras, epilogue, *, tm, tn, n_out, out_dtype, pairs=None, name):
    m = xs[0][0].shape[0]
    pairs = pairs or [(i, i) for i in range(len(ws))]
    cast = ws[0][0].dtype != BF16
    in_specs = []
    args = []
    for a, kb, ki in xs:
        in_specs.append(pl.BlockSpec((tm, kb), lambda n, i, ki=ki: (i, ki)))
        args.append(a)
    for a, kb, ki, off in ws:
        in_specs.append(pl.BlockSpec((kb, tn), lambda n, i, ki=ki, off=off: (ki, n + off)))
        args.append(a)
    for a, kind, off in extras:
        if kind == "tile":
            in_specs.append(pl.BlockSpec((tm, tn), lambda n, i, off=off: (i, n + off)))
        else:
            in_specs.append(pl.BlockSpec((1, tn), lambda n, i, off=off: (0, n + off)))
        args.append(a)
    scratch = [pltpu.VMEM((kb, tn), BF16) for _, kb, _, _ in ws] if cast else []
    cast_rows = LANES
    for _, kb, _, _ in ws:
        assert kb % cast_rows == 0 or not cast, kb
    kern = functools.partial(_ws_kernel, nx=len(xs), nw=len(ws), ne=len(extras), pairs=pairs, cast=cast,
                             epilogue=epilogue, cast_rows=cast_rows)
    return pl.pallas_call(
        kern,
        grid=(pl.cdiv(n_out, tn), m // tm),
        in_specs=in_specs,
        out_specs=pl.BlockSpec((tm, tn), lambda n, i: (i, n)),
        out_shape=jax.ShapeDtypeStruct((m, n_out), out_dtype),
        scratch_shapes=scratch,
        compiler_params=_params("parallel", "arbitrary"),
        name=name,
    )(*args)


def _ep_plain(dots, extras):
    return dots[0]


def _ep_bias_sigmoid(dots, extras):
    return jax.nn.sigmoid(dots[0] + extras[0])


def _ep_residual(dots, extras):
    return extras[0] + dots[0]


def _ep_swiglu(dots, extras):
    g, u = dots
    return g * jax.nn.sigmoid(g) * u


def _ep_merge(dots, extras):
    return (extras[0].astype(F32) * dots[0] + extras[1].astype(F32) * dots[1]
            + extras[2].astype(F32) * dots[2])


def _log_sigmoid(x):
    return jnp.minimum(x, 0.0) - jnp.log1p(jnp.exp(-jnp.abs(x)))


def _cumsum_lanes(x, seg):
    n = x.shape[-1]
    pos = lax.broadcasted_iota(jnp.int32, x.shape, 1) % seg
    shift = 1
    while shift < seg:
        x = x + jnp.where(pos >= shift, pltpu.roll(x, shift, axis=1), 0.0)
        shift *= 2
    del n
    return x


def _gates_kernel(g_ref, bias_ref, rows_ref, *, chunk):
    g = g_ref[...] + bias_ref[...]
    gt = g.T
    s = gt.shape[1]
    rows_ref[0:8, :] = _cumsum_lanes(_log_sigmoid(gt[64:72, :]), s)
    rows_ref[8:16, :] = gt[72:80, :]
    rows_ref[16:24, :] = _cumsum_lanes(_log_sigmoid(gt[80:88, :]), chunk)


def _gate_rows(seg3, bias_vec, b, s):
    m, w = seg3.shape
    return pl.pallas_call(
        functools.partial(_gates_kernel, chunk=ML_CHUNK),
        grid=(b,),
        in_specs=[pl.BlockSpec((s, LANES), lambda i: (i, w // LANES - 1)),
                  pl.BlockSpec((1, LANES), lambda i: (0, 0))],
        out_specs=pl.BlockSpec((None, 24, s), lambda i: (i, 0, 0)),
        out_shape=jax.ShapeDtypeStruct((b, 24, s), F32),
        compiler_params=_params("parallel"),
        name="gate_rows",
    )(seg3, bias_vec)


def _fox_prep_kernel(x_ref, qg_ref, kg_ref, o_ref):
    scale = FOX_DIM ** -0.5
    for h in range(2 * FOX_HEADS):
        x = x_ref[:, h * FOX_DIM:(h + 1) * FOX_DIM].astype(F32)
        y = x * lax.rsqrt(jnp.mean(x * x, axis=-1, keepdims=True) + NORM_EPS)
        if h < FOX_HEADS:
            y = y * qg_ref[...] * scale
        else:
            y = y * kg_ref[...]
        o_ref[:, h * FOX_DIM:(h + 1) * FOX_DIM] = y.astype(o_ref.dtype)


def _fox_prep(proj12, q_gain, k_gain, tm=512):
    m = proj12.shape[0]
    w = 2 * FOX_HEADS * FOX_DIM
    return pl.pallas_call(
        _fox_prep_kernel,
        grid=(m // tm,),
        in_specs=[pl.BlockSpec((tm, w), lambda i: (i, 0)),
                  pl.BlockSpec((1, FOX_DIM), lambda i: (0, 0)),
                  pl.BlockSpec((1, FOX_DIM), lambda i: (0, 0))],
        out_specs=pl.BlockSpec((tm, w), lambda i: (i, 0)),
        out_shape=jax.ShapeDtypeStruct((m, w), BF16),
        compiler_params=_params("parallel"),
        name="fox_prep",
    )(proj12, q_gain.reshape(1, -1), k_gain.reshape(1, -1))


def _flash_kernel(*refs, blk, has_bias):
    if has_bias:
        q_ref, k_ref, v_ref, d_ref, o_ref = refs
    else:
        q_ref, k_ref, v_ref, o_ref = refs
        d_ref = None
    h = pl.program_id(1)
    qi = pl.program_id(2)
    q = q_ref[...]
    dv = v_ref.shape[-1]

    def step(j, carry, masked):
        m_prev, l_prev, acc = carry
        c0 = pl.multiple_of(j * blk, blk)
        k = k_ref[pl.ds(c0, blk), :]
        v = v_ref[pl.ds(c0, blk), :]
        s = lax.dot_general(q, k, (((1,), (1,)), ((), ())), preferred_element_type=F32)
        if has_bias:
            s = s - d_ref[pl.ds(h, 1), pl.ds(c0, blk)]
        if masked:
            row = lax.broadcasted_iota(jnp.int32, (blk, blk), 0)
            col = lax.broadcasted_iota(jnp.int32, (blk, blk), 1)
            s = jnp.where(col <= row, s, -jnp.inf)
        m_new = jnp.maximum(m_prev, jnp.max(s, axis=-1, keepdims=True))
        p = jnp.exp(s - m_new)
        alpha = jnp.exp(m_prev - m_new)
        l_new = alpha * l_prev + jnp.sum(p, axis=-1, keepdims=True)
        acc = alpha * acc + jnp.dot(p.astype(BF16), v, preferred_element_type=F32)
        return m_new, l_new, acc

    init = (jnp.full((blk, 1), -jnp.inf, F32), jnp.zeros((blk, 1), F32), jnp.zeros((blk, dv), F32))
    carry = lax.fori_loop(0, qi, lambda j, c: step(j, c, False), init)
    _, l_fin, acc = step(qi, carry, True)
    o_ref[...] = (acc / l_fin).astype(o_ref.dtype)


def _flash(q_arr, q_off, k_arr, k_off, v_arr, v_off, dk, rows, n_heads, out_width):
    b, s, _ = q_arr.shape
    blk = ATT_BLOCK
    in_specs = [
        pl.BlockSpec((None, blk, dk), lambda bi, h, qi: (bi, qi, q_off + h)),
        pl.BlockSpec((None, s, dk), lambda bi, h, qi: (bi, 0, k_off + h)),
        pl.BlockSpec((None, s, LANES), lambda bi, h, qi: (bi, 0, v_off + h)),
    ]
    args = [q_arr, k_arr, v_arr]
    if rows is not None:
        in_specs.append(pl.BlockSpec((None, 8, s), lambda bi, h, qi: (bi, 0, 0)))
        args.append(rows)
    return pl.pallas_call(
        functools.partial(_flash_kernel, blk=blk, has_bias=rows is not None),
        grid=(b, n_heads, s // blk),
        in_specs=in_specs,
        out_specs=pl.BlockSpec((None, blk, LANES), lambda bi, h, qi: (bi, qi, h)),
        out_shape=jax.ShapeDtypeStruct((b, s, out_width), BF16),
        compiler_params=_params("parallel", "parallel", "arbitrary"),
        name="flash_bias" if rows is not None else "flash",
    )(*args)


def _conv_kernel(x_ref, halo_ref, w_ref, b_ref, o_ref, xs_ref):
    tm = x_ref.shape[0]
    first = pl.program_id(1) == 0
    halo = halo_ref[...].astype(F32)
    xs_ref[0:HALO, :] = jnp.where(first, 0.0, halo)
    xs_ref[HALO:, :] = x_ref[...].astype(F32)
    y = b_ref[...]
    for j in range(ML_CONV):
        off = HALO - (ML_CONV - 1) + j
        y = y + w_ref[j:j + 1, :] * xs_ref[off:off + tm, :]
    o_ref[...] = (y * jax.nn.sigmoid(y)).astype(o_ref.dtype)


def _mlstm_conv(proj12_3d, col_block, conv_w, conv_b, tm=256):
    b, s, _ = proj12_3d.shape
    c = conv_w.shape[1]
    hb = tm // HALO
    return pl.pallas_call(
        _conv_kernel,
        grid=(b, s // tm),
        in_specs=[pl.BlockSpec((None, tm, c), lambda bi, i: (bi, i, col_block)),
                  pl.BlockSpec((None, HALO, c), lambda bi, i: (bi, jnp.maximum(i * hb - 1, 0), col_block)),
                  pl.BlockSpec((ML_CONV, c), lambda bi, i: (0, 0)),
                  pl.BlockSpec((1, c), lambda bi, i: (0, 0))],
        out_specs=pl.BlockSpec((None, tm, c), lambda bi, i: (bi, i, 0)),
        out_shape=jax.ShapeDtypeStruct((b, s, c), BF16),
        scratch_shapes=[pltpu.VMEM((tm + HALO, c), F32)],
        compiler_params=_params("parallel", "parallel"),
        name="mlstm_conv",
    )(proj12_3d, proj12_3d, conv_w, conv_b.reshape(1, c))


def _mlstm_kernel(q_ref, k_ref, v_ref, og_ref, rows_ref, gain_ref, o_ref, c_ref, n_ref, m_ref, *, chunk):
    L = chunk
    hp = pl.program_id(1)
    s = q_ref.shape[0]
    nc = s // L
    c_ref[...] = jnp.zeros_like(c_ref)
    n_ref[...] = jnp.zeros_like(n_ref)
    m_ref[...] = jnp.zeros_like(m_ref)

    lane = lax.broadcasted_iota(jnp.int32, (1, LANES), 1)
    row = lax.broadcasted_iota(jnp.int32, (L, L), 0)
    col = lax.broadcasted_iota(jnp.int32, (L, L), 1)
    causal = col <= row
    eye = col == row
    col1 = lax.broadcasted_iota(jnp.int32, (1, L), 1)
    row1 = lax.broadcasted_iota(jnp.int32, (L, 1), 0)

    def to_col(r):
        return jnp.sum(jnp.where(eye, r, 0.0), axis=-1, keepdims=True)

    def chunk_step(ci, carry):
        t0 = pl.multiple_of(ci * L, L)
        qp = q_ref[pl.ds(t0, L), :]
        kp = k_ref[pl.ds(t0, L), :].astype(F32) * (ML_QK ** -0.5)
        kp_b = kp.astype(BF16)
        for hh in range(2):
            head = 2 * hp + hh
            li = rows_ref[pl.ds(8 + head, 1), pl.ds(t0, L)]
            bc = rows_ref[pl.ds(16 + head, 1), pl.ds(t0, L)]
            m_prev = m_ref[hh:hh + 1, 0:1]
            a_row = li - bc
            bc_col = to_col(bc)
            log_w = jnp.where(causal, bc_col + a_row, -jnp.inf)
            log_inter = bc_col + m_prev
            m_t = jnp.maximum(log_inter, jnp.max(log_w, axis=-1, keepdims=True))
            w = jnp.exp(log_w - m_t)
            inter = jnp.exp(log_inter - m_t)
            head_lanes = (lane >= hh * ML_QK) & (lane < (hh + 1) * ML_QK)
            qh = jnp.where(head_lanes, qp, jnp.zeros_like(qp))
            qk = lax.dot_general(qh, kp_b, (((1,), (1,)), ((), ())), preferred_element_type=F32) * w
            vh = v_ref[pl.ds(t0, L), hh * ML_V:(hh + 1) * ML_V]
            c_state = c_ref[hh]
            n_state = n_ref[hh:hh + 1, :]
            num = (jnp.dot(qk.astype(BF16), vh, preferred_element_type=F32)
                   + inter * jnp.dot(qh, c_state.astype(BF16), preferred_element_type=F32))
            den = (jnp.sum(qk, axis=-1, keepdims=True)
                   + inter * jnp.sum(qh.astype(F32) * n_state, axis=-1, keepdims=True))
            h_out = num / jnp.maximum(jnp.abs(den), jnp.exp(-m_t))
            hn = h_out * lax.rsqrt(jnp.mean(h_out * h_out, axis=-1, keepdims=True) + NORM_EPS)
            hn = hn * gain_ref[:, hh * ML_V:(hh + 1) * ML_V]
            og = og_ref[pl.ds(t0, L), hh * ML_V:(hh + 1) * ML_V].astype(F32)
            o_ref[pl.ds(t0, L), hh * ML_V:(hh + 1) * ML_V] = (hn * jax.nn.sigmoid(og)).astype(o_ref.dtype)
            m_new = jnp.sum(jnp.where(row1 == L - 1, m_t, 0.0), axis=0, keepdims=True)
            b_last = jnp.sum(jnp.where(col1 == L - 1, bc, 0.0), axis=-1, keepdims=True)
            decay = jnp.exp(b_last + m_prev - m_new)
            w_last = to_col(jnp.exp(b_last + a_row - m_new))
            kw = kp * w_last
            c_upd = lax.dot_general(kw.astype(BF16), vh, (((0,), (0,)), ((), ())), preferred_element_type=F32)
            c_ref[hh] = decay * c_state + c_upd
            n_ref[hh:hh + 1, :] = decay * n_state + jnp.sum(kw, axis=0, keepdims=True)
            m_ref[hh:hh + 1, :] = jnp.broadcast_to(m_new, (1, LANES))
        return carry

    lax.fori_loop(0, nc, chunk_step, 0)


def _mlstm(qk_conv, proj12_3d, v_blk0, og_blk0, rows, out_gain):
    b, s, _ = qk_conv.shape
    npair = ML_HEADS // 2
    return pl.pallas_call(
        functools.partial(_mlstm_kernel, chunk=ML_CHUNK),
        grid=(b, npair),
        in_specs=[pl.BlockSpec((None, s, LANES), lambda bi, hp: (bi, 0, hp)),
                  pl.BlockSpec((None, s, LANES), lambda bi, hp: (bi, 0, npair + hp)),
                  pl.BlockSpec((None, s, 2 * ML_V), lambda bi, hp: (bi, 0, v_blk0 + hp)),
                  pl.BlockSpec((None, s, 2 * ML_V), lambda bi, hp: (bi, 0, og_blk0 + hp)),
                  pl.BlockSpec((None, 24, s), lambda bi, hp: (bi, 0, 0)),
                  pl.BlockSpec((1, 2 * ML_V), lambda bi, hp: (0, hp))],
        out_specs=pl.BlockSpec((None, s, 2 * ML_V), lambda bi, hp: (bi, 0, hp)),
        out_shape=jax.ShapeDtypeStruct((b, s, ML_HEADS * ML_V), BF16),
        scratch_shapes=[pltpu.VMEM((2, LANES, ML_V), F32),
                        pltpu.VMEM((8, LANES), F32),
                        pltpu.VMEM((8, LANES), F32)],
        compiler_params=_params("parallel", "parallel"),
        name="mlstm",
    )(qk_conv, qk_conv, proj12_3d, proj12_3d, rows, out_gain.reshape(1, -1))


def _mla_prep_kernel(s_ref, cos_ref, sin_ref, wuq_ref, wukv_ref, cqg_ref, ckvg_ref, qgn_ref, qgr_ref,
                     kgn_ref, kgr_ref, q_ref, k_ref, v_ref):
    scale = MLA_QK ** -0.5
    lane = lax.broadcasted_iota(jnp.int32, (1, LANES), 1)
    cosf = cos_ref[...]
    sinf = sin_ref[...]

    def norm(x, g):
        return x * lax.rsqrt(jnp.mean(x * x, axis=-1, keepdims=True) + NORM_EPS) * g

    def rope(x):
        swap = jnp.where(lane < MLA_ROPE // 2, pltpu.roll(x, LANES - MLA_ROPE // 2, axis=1),
                         pltpu.roll(x, MLA_ROPE // 2, axis=1))
        return x * cosf + swap * sinf

    cq = s_ref[:, 0:MLA_Q_RANK]
    ckv = s_ref[:, MLA_Q_RANK:MLA_Q_RANK + MLA_KV_RANK]
    kr = jnp.where(lane < MLA_ROPE, s_ref[:, MLA_Q_RANK + MLA_KV_RANK:], 0.0)
    qf = jnp.dot(norm(cq, cqg_ref[...]).astype(BF16), wuq_ref[...], preferred_element_type=F32)
    kvf = jnp.dot(norm(ckv, ckvg_ref[...]).astype(BF16), wukv_ref[...], preferred_element_type=F32)
    ss_kr = jnp.sum(kr * kr, axis=-1, keepdims=True)
    kr_g = kr * kgr_ref[...]
    nw = MLA_HEADS * MLA_NOPE
    for h in range(MLA_HEADS):
        qn = qf[:, h * MLA_NOPE:(h + 1) * MLA_NOPE]
        qr = qf[:, nw + h * LANES:nw + (h + 1) * LANES]
        ss = jnp.sum(qn * qn, axis=-1, keepdims=True) + jnp.sum(qr * qr, axis=-1, keepdims=True)
        r = lax.rsqrt(ss / MLA_QK + NORM_EPS)
        q_ref[:, h * MLA_PAD:h * MLA_PAD + MLA_NOPE] = (qn * r * qgn_ref[...] * scale).astype(q_ref.dtype)
        q_ref[:, h * MLA_PAD + MLA_NOPE:(h + 1) * MLA_PAD] = (rope(qr * r * qgr_ref[...]) * scale).astype(q_ref.dtype)
        kn = kvf[:, h * 2 * MLA_NOPE:h * 2 * MLA_NOPE + MLA_NOPE]
        rk = lax.rsqrt((jnp.sum(kn * kn, axis=-1, keepdims=True) + ss_kr) / MLA_QK + NORM_EPS)
        k_ref[:, h * MLA_PAD:h * MLA_PAD + MLA_NOPE] = (kn * rk * kgn_ref[...]).astype(k_ref.dtype)
        k_ref[:, h * MLA_PAD + MLA_NOPE:(h + 1) * MLA_PAD] = rope(kr_g * rk).astype(k_ref.dtype)
        v_ref[:, h * MLA_V:(h + 1) * MLA_V] = kvf[:, h * 2 * MLA_NOPE + MLA_NOPE:(h + 1) * 2 * MLA_NOPE].astype(v_ref.dtype)


def _mla_prep(seg3, cosf, sinf, wuq_p, wukv_b, cq_gain, ckv_gain, q_gain, k_gain, tm=256):
    m, w3 = seg3.shape
    pad = jnp.zeros((LANES - MLA_ROPE,), F32)
    qgn = q_gain[:MLA_NOPE].reshape(1, -1)
    qgr = jnp.concatenate([q_gain[MLA_NOPE:], pad]).reshape(1, -1)
    kgn = k_gain[:MLA_NOPE].reshape(1, -1)
    kgr = jnp.concatenate([k_gain[MLA_NOPE:], pad]).reshape(1, -1)
    const = lambda i: (0, 0)
    return pl.pallas_call(
        _mla_prep_kernel,
        grid=(m // tm,),
        in_specs=[pl.BlockSpec((tm, w3), lambda i: (i, 0)),
                  pl.BlockSpec((tm, LANES), lambda i: (i, 0)),
                  pl.BlockSpec((tm, LANES), lambda i: (i, 0)),
                  pl.BlockSpec(wuq_p.shape, const),
                  pl.BlockSpec(wukv_b.shape, const),
                  pl.BlockSpec((1, MLA_Q_RANK), const),
                  pl.BlockSpec((1, MLA_KV_RANK), const),
                  pl.BlockSpec((1, LANES), const),
                  pl.BlockSpec((1, LANES), const),
                  pl.BlockSpec((1, LANES), const),
                  pl.BlockSpec((1, LANES), const)],
        out_specs=[pl.BlockSpec((tm, MLA_HEADS * MLA_PAD), lambda i: (i, 0)),
                   pl.BlockSpec((tm, MLA_HEADS * MLA_PAD), lambda i: (i, 0)),
                   pl.BlockSpec((tm, MLA_HEADS * MLA_V), lambda i: (i, 0))],
        out_shape=[jax.ShapeDtypeStruct((m, MLA_HEADS * MLA_PAD), BF16),
                   jax.ShapeDtypeStruct((m, MLA_HEADS * MLA_PAD), BF16),
                   jax.ShapeDtypeStruct((m, MLA_HEADS * MLA_V), BF16)],
        compiler_params=_params("parallel"),
        name="mla_prep",
    )(seg3, cosf, sinf, wuq_p, wukv_b, cq_gain.reshape(1, -1), ckv_gain.reshape(1, -1), qgn, qgr, kgn, kgr)


def _pack_w_in(w_in):
    fw = FOX_HEADS * FOX_DIM
    sizes = (fw, fw, fw, FOX_HEADS, ML_HEADS * ML_QK, ML_HEADS * ML_QK, ML_HEADS * ML_V,
             ML_HEADS, ML_HEADS, ML_HEADS * ML_V, MLA_Q_RANK, MLA_KV_RANK, MLA_ROPE)
    offs = [0]
    for sz in sizes:
        offs.append(offs[-1] + sz)
    seg = lambda i: w_in[:, offs[i]:offs[i + 1]]
    d = w_in.shape[0]
    w12 = jnp.concatenate([seg(0), seg(1), seg(2), seg(4), seg(5), seg(6), seg(9)], axis=1).astype(BF16)
    small = jnp.concatenate([seg(12), seg(3), seg(7), seg(8),
                             jnp.zeros((d, LANES - MLA_ROPE - FOX_HEADS - 2 * ML_HEADS), w_in.dtype)], axis=1)
    w3 = jnp.concatenate([seg(10), seg(11), small], axis=1).astype(BF16)
    wg = w_in[:, offs[-1]:].astype(BF16)
    return w12, w3, wg


def _layer(x, cosf, sinf, b, s, p):
    m, d = x.shape
    w12, w3, wg = _pack_w_in(p["w_in"])
    h1 = _rmsnorm(x, p["mix_norm"])

    proj12 = _ws_matmul([(h1, d, 0)], [(w12, d, 0, 0)], [], _ep_plain, tm=1024, tn=1024,
                        n_out=w12.shape[1], out_dtype=BF16, name="proj12")
    seg3 = _ws_matmul([(h1, d, 0)], [(w3, d, 0, 0)], [], _ep_plain, tm=1024, tn=512,
                      n_out=w3.shape[1], out_dtype=F32, name="proj3")
    gates = _ws_matmul([(h1, d, 0)], [(wg, d, 0, 0)], [(p["gate_bias"].reshape(1, -1), "row", 0)],
                       _ep_bias_sigmoid, tm=1024, tn=1024, n_out=wg.shape[1], out_dtype=BF16, name="gates")

    bias_vec = jnp.concatenate([jnp.zeros((MLA_ROPE,), F32), p["fox_f_bias"], p["mlstm_i_bias"], p["mlstm_f_bias"],
                                jnp.zeros((LANES - MLA_ROPE - 24,), F32)]).reshape(1, LANES)
    rows = _gate_rows(seg3, bias_vec, b, s)

    proj12_3d = proj12.reshape(b, s, -1)
    qk_fox = _fox_prep(proj12, p["fox_q_gain"], p["fox_k_gain"]).reshape(b, s, -1)
    o_fox = _flash(qk_fox, 0, qk_fox, FOX_HEADS, proj12_3d, 2 * FOX_HEADS, FOX_DIM, rows, FOX_HEADS,
                   FOX_HEADS * FOX_DIM).reshape(m, -1)
    qk_conv = _mlstm_conv(proj12_3d, 3, p["mlstm_conv_w"], p["mlstm_conv_b"])
    o_ml = _mlstm(qk_conv, proj12_3d, 4096 // (2 * ML_V), 5120 // (2 * ML_V), rows,
                  p["mlstm_out_gain"]).reshape(m, -1)
    wuq = p["mla_w_uq"].reshape(MLA_Q_RANK, MLA_HEADS, MLA_QK)
    wuq_p = jnp.concatenate([
        wuq[:, :, :MLA_NOPE].reshape(MLA_Q_RANK, -1),
        jnp.pad(wuq[:, :, MLA_NOPE:], ((0, 0), (0, 0), (0, LANES - MLA_ROPE))).reshape(MLA_Q_RANK, -1)],
        axis=1).astype(BF16)
    q_mla, k_mla, v_mla = _mla_prep(seg3, cosf, sinf, wuq_p, p["mla_w_ukv"].astype(BF16), p["mla_cq_gain"],
                                    p["mla_ckv_gain"], p["mla_q_gain"], p["mla_k_gain"])
    o_mla = _flash(q_mla.reshape(b, s, -1), 0, k_mla.reshape(b, s, -1), 0, v_mla.reshape(b, s, -1), 0,
                   MLA_PAD, None, MLA_HEADS, MLA_HEADS * MLA_V).reshape(m, -1)

    kb = o_fox.shape[1]
    nb = d // 512
    merged = _ws_matmul([(o_fox, kb, 0), (o_ml, kb, 0), (o_mla, kb, 0)],
                        [(p["w_fox_out"], kb, 0, 0), (p["w_mlstm_out"], kb, 0, 0), (p["w_mla_out"], kb, 0, 0)],
                        [(gates, "tile", 0), (gates, "tile", nb), (gates, "tile", 2 * nb)],
                        _ep_merge, tm=1024, tn=512, n_out=d, out_dtype=BF16, name="merge")
    x = _ws_matmul([(merged, d, 0)], [(p["w_o"], d, 0, 0)], [(x, "tile", 0)], _ep_residual,
                   tm=1024, tn=512, n_out=d, out_dtype=F32, name="w_o")

    h2 = _rmsnorm(x, p["ffn_norm"])
    d_ff = p["w_gate"].shape[1]
    act = _ws_matmul([(h2, d, 0)], [(p["w_gate"], d, 0, 0), (p["w_up"], d, 0, 0)], [], _ep_swiglu,
                     tm=1024, tn=256, n_out=d_ff, out_dtype=BF16, pairs=[(0, 0), (0, 1)], name="gate_up")
    kh = d_ff // 2
    for ki in range(2):
        x = _ws_matmul([(act, kh, ki)], [(p["w_down"], kh, ki, 0)], [(x, "tile", 0)], _ep_residual,
                       tm=512, tn=512, n_out=d, out_dtype=F32, name=f"down{ki}")
    return x


def kernel(x, positions, mix_norm, w_in, fox_f_bias, fox_q_gain, fox_k_gain, mlstm_conv_w, mlstm_conv_b,
           mlstm_i_bias, mlstm_f_bias, mlstm_out_gain, mla_cq_gain, mla_ckv_gain, mla_w_uq, mla_w_ukv,
           mla_q_gain, mla_k_gain, w_fox_out, w_mlstm_out, w_mla_out, gate_bias, w_o, ffn_norm, w_gate,
           w_up, w_down):
    b, s, d = x.shape
    depth = w_in.shape[0]
    inv_freq = jnp.power(ROPE_THETA, -jnp.arange(0, MLA_ROPE, 2, dtype=F32) / MLA_ROPE)
    ang = positions.astype(F32).reshape(b * s, 1) * inv_freq
    zpad = jnp.zeros((b * s, LANES - MLA_ROPE), F32)
    cosf = jnp.concatenate([jnp.cos(ang), jnp.cos(ang), zpad], axis=1)
    sinf = jnp.concatenate([-jnp.sin(ang), jnp.sin(ang), zpad], axis=1)
    names = dict(mix_norm=mix_norm, w_in=w_in, fox_f_bias=fox_f_bias, fox_q_gain=fox_q_gain, fox_k_gain=fox_k_gain,
                 mlstm_conv_w=mlstm_conv_w, mlstm_conv_b=mlstm_conv_b, mlstm_i_bias=mlstm_i_bias,
                 mlstm_f_bias=mlstm_f_bias, mlstm_out_gain=mlstm_out_gain, mla_cq_gain=mla_cq_gain,
                 mla_ckv_gain=mla_ckv_gain, mla_w_uq=mla_w_uq, mla_w_ukv=mla_w_ukv, mla_q_gain=mla_q_gain,
                 mla_k_gain=mla_k_gain, w_fox_out=w_fox_out, w_mlstm_out=w_mlstm_out, w_mla_out=w_mla_out,
                 gate_bias=gate_bias, w_o=w_o, ffn_norm=ffn_norm, w_gate=w_gate, w_up=w_up, w_down=w_down)
    xf = x.reshape(b * s, d)
    for l in range(depth):
        xf = _layer(xf, cosf, sinf, b, s, {k: v[l] for k, v in names.items()})
    return xf.reshape(b, s, d)
```

```python
import functools

import jax
import jax.numpy as jnp
from jax import lax
from jax.experimental import pallas as pl
from jax.experimental.pallas import tpu as pltpu

F32 = jnp.float32
BF16 = jnp.bfloat16

NORM_EPS = 1e-6
ROPE_THETA = 10000.0

FOX_HEADS = 8
FOX_DIM = 128
ML_HEADS = 8
ML_QK = 64
ML_V = 128
ML_CONV = 4
MLA_HEADS = 8
MLA_Q_RANK = 896
MLA_KV_RANK = 512
MLA_NOPE = 128
MLA_ROPE = 64
MLA_V = 128
MLA_QK = MLA_NOPE + MLA_ROPE
MLA_PAD = 256

LANES = 128
VMEM_LIMIT = 56 * 1024 * 1024
ML_CHUNK = 256
ATT_BLOCK = 256
HALO = 16


def _params(*sem):
    return pltpu.CompilerParams(dimension_semantics=sem, vmem_limit_bytes=VMEM_LIMIT)


def _rmsnorm_kernel(x_ref, g_ref, o_ref):
    x = x_ref[...]
    ms = jnp.mean(x * x, axis=-1, keepdims=True)
    o_ref[...] = (x * lax.rsqrt(ms + NORM_EPS) * g_ref[...]).astype(o_ref.dtype)


def _rmsnorm(x, gain, tm=256):
    m, d = x.shape
    return pl.pallas_call(
        _rmsnorm_kernel,
        grid=(m // tm,),
        in_specs=[pl.BlockSpec((tm, d), lambda i: (i, 0)),
                  pl.BlockSpec((1, d), lambda i: (0, 0))],
        out_specs=pl.BlockSpec((tm, d), lambda i: (i, 0)),
        out_shape=jax.ShapeDtypeStruct((m, d), BF16),
        compiler_params=_params("parallel"),
        name="rmsnorm",
    )(x, gain.reshape(1, d))


def _ws_kernel(*refs, nx, nw, ne, pairs, cast, epilogue, cast_rows):
    x_refs = refs[:nx]
    w_refs = refs[nx:nx + nw]
    e_refs = refs[nx + nw:nx + nw + ne]
    o_ref = refs[nx + nw + ne]
    s_refs = refs[nx + nw + ne + 1:]
    if cast:
        @pl.when(pl.program_id(1) == 0)
        def _():
            for w_ref, s_ref in zip(w_refs, s_refs):
                k = w_ref.shape[0]

                def body(i, c, w_ref=w_ref, s_ref=s_ref):
                    r = pl.multiple_of(i * cast_rows, cast_rows)
                    s_ref[pl.ds(r, cast_rows), :] = w_ref[pl.ds(r, cast_rows), :].astype(BF16)
                    return c

                lax.fori_loop(0, k // cast_rows, body, 0)
        w_src = s_refs
    else:
        w_src = w_refs
    dots = [jnp.dot(x_refs[xi][...], w_src[wi][...], preferred_element_type=F32) for xi, wi in pairs]
    o_ref[...] = epilogue(dots, [e[...] for e in e_refs]).astype(o_ref.dtype)


def _ws_matmul(xs, ws, extras, epilogue, *, tm, tn, n_out, out_dtype, pairs=None, name):
    m = xs[0][0].shape[0]
    pairs = pairs or [(i, i) for i in range(len(ws))]
    first_w = ws[0][0][0] if isinstance(ws[0][0], tuple) else ws[0][0]
    cast = first_w.dtype != BF16
    in_specs = []
    args = []
    for a, kb, ki in xs:
        in_specs.append(pl.BlockSpec((tm, kb), lambda n, i, ki=ki: (i, ki)))
        args.append(a)
    for a, kb, ki, off in ws:
        if isinstance(a, tuple):
            a, layer = a
            in_specs.append(pl.BlockSpec((None, kb, tn), lambda n, i, ki=ki, off=off, layer=layer: (layer, ki, n + off)))
        else:
            in_specs.append(pl.BlockSpec((kb, tn), lambda n, i, ki=ki, off=off: (ki, n + off)))
        args.append(a)
    for a, kind, off in extras:
        if kind == "tile":
            in_specs.append(pl.BlockSpec((tm, tn), lambda n, i, off=off: (i, n + off)))
        else:
            in_specs.append(pl.BlockSpec((1, tn), lambda n, i, off=off: (0, n + off)))
        args.append(a)
    scratch = [pltpu.VMEM((kb, tn), BF16) for _, kb, _, _ in ws] if cast else []
    cast_rows = LANES
    for _, kb, _, _ in ws:
        assert kb % cast_rows == 0 or not cast, kb
    kern = functools.partial(_ws_kernel, nx=len(xs), nw=len(ws), ne=len(extras), pairs=pairs, cast=cast,
                             epilogue=epilogue, cast_rows=cast_rows)
    return pl.pallas_call(
        kern,
        grid=(pl.cdiv(n_out, tn), m // tm),
        in_specs=in_specs,
        out_specs=pl.BlockSpec((tm, tn), lambda n, i: (i, n)),
        out_shape=jax.ShapeDtypeStruct((m, n_out), out_dtype),
        scratch_shapes=scratch,
        compiler_params=_params("parallel", "arbitrary"),
        name=name,
    )(*args)


def _ep_plain(dots, extras):
    return dots[0]


def _ep_bias_sigmoid(dots, extras):
    return jax.nn.sigmoid(dots[0] + extras[0])


def _ep_residual(dots, extras):
    return extras[0] + dots[0]


def _ep_swiglu(dots, extras):
    g, u = dots
    return g * jax.nn.sigmoid(g) * u


def _ep_merge(dots, extras):
    return (extras[0].astype(F32) * dots[0] + extras[1].astype(F32) * dots[1]
            + extras[2].astype(F32) * dots[2])


def _log_sigmoid(x):
    return jnp.minimum(x, 0.0) - jnp.log1p(jnp.exp(-jnp.abs(x)))


def _cumsum_lanes(x, seg):
    n = x.shape[-1]
    pos = lax.broadcasted_iota(jnp.int32, x.shape, 1) % seg
    shift = 1
    while shift < seg:
        x = x + jnp.where(pos >= shift, pltpu.roll(x, shift, axis=1), 0.0)
        shift *= 2
    del n
    return x


def _gates_kernel(g_ref, bias_ref, rows_ref, *, chunk):
    g = g_ref[...] + bias_ref[...]
    gt = g.T
    s = gt.shape[1]
    rows_ref[0:8, :] = _cumsum_lanes(_log_sigmoid(gt[64:72, :]), s)
    rows_ref[8:16, :] = gt[72:80, :]
    rows_ref[16:24, :] = _cumsum_lanes(_log_sigmoid(gt[80:88, :]), chunk)


def _gate_rows(seg3, bias_vec, b, s):
    m, w = seg3.shape
    return pl.pallas_call(
        functools.partial(_gates_kernel, chunk=ML_CHUNK),
        grid=(b,),
        in_specs=[pl.BlockSpec((s, LANES), lambda i: (i, w // LANES - 1)),
                  pl.BlockSpec((1, LANES), lambda i: (0, 0))],
        out_specs=pl.BlockSpec((None, 24, s), lambda i: (i, 0, 0)),
        out_shape=jax.ShapeDtypeStruct((b, 24, s), F32),
        compiler_params=_params("parallel"),
        name="gate_rows",
    )(seg3, bias_vec)


def _fox_prep_kernel(x_ref, qg_ref, kg_ref, o_ref):
    scale = FOX_DIM ** -0.5
    for h in range(2 * FOX_HEADS):
        x = x_ref[:, h * FOX_DIM:(h + 1) * FOX_DIM].astype(F32)
        y = x * lax.rsqrt(jnp.mean(x * x, axis=-1, keepdims=True) + NORM_EPS)
        if h < FOX_HEADS:
            y = y * qg_ref[...] * scale
        else:
            y = y * kg_ref[...]
        o_ref[:, h * FOX_DIM:(h + 1) * FOX_DIM] = y.astype(o_ref.dtype)


def _fox_prep(proj12, q_gain, k_gain, tm=512):
    m = proj12.shape[0]
    w = 2 * FOX_HEADS * FOX_DIM
    return pl.pallas_call(
        _fox_prep_kernel,
        grid=(m // tm,),
        in_specs=[pl.BlockSpec((tm, w), lambda i: (i, 0)),
                  pl.BlockSpec((1, FOX_DIM), lambda i: (0, 0)),
                  pl.BlockSpec((1, FOX_DIM), lambda i: (0, 0))],
        out_specs=pl.BlockSpec((tm, w), lambda i: (i, 0)),
        out_shape=jax.ShapeDtypeStruct((m, w), BF16),
        compiler_params=_params("parallel"),
        name="fox_prep",
    )(proj12, q_gain.reshape(1, -1), k_gain.reshape(1, -1))


def _flash_kernel(*refs, blk, has_bias):
    if has_bias:
        q_ref, k_ref, v_ref, d_ref, o_ref = refs
    else:
        q_ref, k_ref, v_ref, o_ref = refs
        d_ref = None
    h = pl.program_id(1)
    s_len = k_ref.shape[0]
    row = lax.broadcasted_iota(jnp.int32, (blk, blk), 0)
    col = lax.broadcasted_iota(jnp.int32, (blk, blk), 1)
    nt = (((1,), (1,)), ((), ()))
    for qi in range(s_len // blk):
        lo = qi * blk
        q = q_ref[lo:lo + blk, :]
        s_diag = lax.dot_general(q, k_ref[lo:lo + blk, :], nt, preferred_element_type=F32)
        if has_bias:
            s_diag = s_diag - d_ref[pl.ds(h, 1), lo:lo + blk]
        s_diag = jnp.where(col <= row, s_diag, -jnp.inf)
        m = jnp.max(s_diag, axis=-1, keepdims=True)
        if qi:
            s_off = lax.dot_general(q, k_ref[0:lo, :], nt, preferred_element_type=F32)
            if has_bias:
                s_off = s_off - d_ref[pl.ds(h, 1), 0:lo]
            m = jnp.maximum(m, jnp.max(s_off, axis=-1, keepdims=True))
        p_diag = jnp.exp(s_diag - m)
        l = jnp.sum(p_diag, axis=-1, keepdims=True)
        acc = jnp.dot(p_diag.astype(BF16), v_ref[lo:lo + blk, :], preferred_element_type=F32)
        if qi:
            p_off = jnp.exp(s_off - m)
            l = l + jnp.sum(p_off, axis=-1, keepdims=True)
            acc = acc + jnp.dot(p_off.astype(BF16), v_ref[0:lo, :], preferred_element_type=F32)
        o_ref[lo:lo + blk, :] = (acc / l).astype(o_ref.dtype)


def _flash(q_arr, q_off, k_arr, k_off, v_arr, v_off, dk, rows, n_heads, out_width):
    b, s, _ = q_arr.shape
    blk = ATT_BLOCK
    in_specs = [
        pl.BlockSpec((None, s, dk), lambda bi, h: (bi, 0, q_off + h)),
        pl.BlockSpec((None, s, dk), lambda bi, h: (bi, 0, k_off + h)),
        pl.BlockSpec((None, s, LANES), lambda bi, h: (bi, 0, v_off + h)),
    ]
    args = [q_arr, k_arr, v_arr]
    if rows is not None:
        in_specs.append(pl.BlockSpec((None, 8, s), lambda bi, h: (bi, 0, 0)))
        args.append(rows)
    return pl.pallas_call(
        functools.partial(_flash_kernel, blk=blk, has_bias=rows is not None),
        grid=(b, n_heads),
        in_specs=in_specs,
        out_specs=pl.BlockSpec((None, s, LANES), lambda bi, h: (bi, 0, h)),
        out_shape=jax.ShapeDtypeStruct((b, s, out_width), BF16),
        compiler_params=_params("parallel", "parallel"),
        name="flash_bias" if rows is not None else "flash",
    )(*args)


def _conv_kernel(x_ref, halo_ref, w_ref, b_ref, o_ref, xs_ref):
    tm = x_ref.shape[0]
    first = pl.program_id(1) == 0
    halo = halo_ref[...].astype(F32)
    xs_ref[0:HALO, :] = jnp.where(first, 0.0, halo)
    xs_ref[HALO:, :] = x_ref[...].astype(F32)
    y = b_ref[...]
    for j in range(ML_CONV):
        off = HALO - (ML_CONV - 1) + j
        y = y + w_ref[j:j + 1, :] * xs_ref[off:off + tm, :]
    o_ref[...] = (y * jax.nn.sigmoid(y)).astype(o_ref.dtype)


def _mlstm_conv(proj12_3d, col_block, conv_w, conv_b, tm=256):
    b, s, _ = proj12_3d.shape
    c = conv_w.shape[1]
    hb = tm // HALO
    return pl.pallas_call(
        _conv_kernel,
        grid=(b, s // tm),
        in_specs=[pl.BlockSpec((None, tm, c), lambda bi, i: (bi, i, col_block)),
                  pl.BlockSpec((None, HALO, c), lambda bi, i: (bi, jnp.maximum(i * hb - 1, 0), col_block)),
                  pl.BlockSpec((ML_CONV, c), lambda bi, i: (0, 0)),
                  pl.BlockSpec((1, c), lambda bi, i: (0, 0))],
        out_specs=pl.BlockSpec((None, tm, c), lambda bi, i: (bi, i, 0)),
        out_shape=jax.ShapeDtypeStruct((b, s, c), BF16),
        scratch_shapes=[pltpu.VMEM((tm + HALO, c), F32)],
        compiler_params=_params("parallel", "parallel"),
        name="mlstm_conv",
    )(proj12_3d, proj12_3d, conv_w, conv_b.reshape(1, c))


def _mlstm_kernel(q_ref, k_ref, v_ref, og_ref, rows_ref, gain_ref, o_ref, c_ref, n_ref, m_ref, *, chunk):
    L = chunk
    hp = pl.program_id(1)
    s = q_ref.shape[0]
    nc = s // L
    c_ref[...] = jnp.zeros_like(c_ref)
    n_ref[...] = jnp.zeros_like(n_ref)
    m_ref[...] = jnp.zeros_like(m_ref)

    lane = lax.broadcasted_iota(jnp.int32, (1, LANES), 1)
    row = lax.broadcasted_iota(jnp.int32, (L, L), 0)
    col = lax.broadcasted_iota(jnp.int32, (L, L), 1)
    causal = col <= row
    eye = col == row
    col1 = lax.broadcasted_iota(jnp.int32, (1, L), 1)
    row1 = lax.broadcasted_iota(jnp.int32, (L, 1), 0)

    def to_col(r):
        return jnp.sum(jnp.where(eye, r, 0.0), axis=-1, keepdims=True)

    def chunk_step(ci, carry):
        t0 = pl.multiple_of(ci * L, L)
        qp = q_ref[pl.ds(t0, L), :]
        kp = k_ref[pl.ds(t0, L), :].astype(F32) * (ML_QK ** -0.5)
        kp_b = kp.astype(BF16)
        for hh in range(2):
            head = 2 * hp + hh
            li = rows_ref[pl.ds(8 + head, 1), pl.ds(t0, L)]
            bc = rows_ref[pl.ds(16 + head, 1), pl.ds(t0, L)]
            m_prev = m_ref[hh:hh + 1, 0:1]
            a_row = li - bc
            bc_col = to_col(bc)
            log_w = jnp.where(causal, bc_col + a_row, -jnp.inf)
            log_inter = bc_col + m_prev
            m_t = jnp.maximum(log_inter, jnp.max(log_w, axis=-1, keepdims=True))
            w = jnp.exp(log_w - m_t)
            inter = jnp.exp(log_inter - m_t)
            head_lanes = (lane >= hh * ML_QK) & (lane < (hh + 1) * ML_QK)
            qh = jnp.where(head_lanes, qp, jnp.zeros_like(qp))
            qk = lax.dot_general(qh, kp_b, (((1,), (1,)), ((), ())), preferred_element_type=F32) * w
            vh = v_ref[pl.ds(t0, L), hh * ML_V:(hh + 1) * ML_V]
            c_state = c_ref[hh]
            n_state = n_ref[hh:hh + 1, :]
            num = (jnp.dot(qk.astype(BF16), vh, preferred_element_type=F32)
                   + inter * jnp.dot(qh, c_state.astype(BF16), preferred_element_type=F32))
            den = (jnp.sum(qk, axis=-1, keepdims=True)
                   + inter * jnp.sum(qh.astype(F32) * n_state, axis=-1, keepdims=True))
            h_out = num / jnp.maximum(jnp.abs(den), jnp.exp(-m_t))
            hn = h_out * lax.rsqrt(jnp.mean(h_out * h_out, axis=-1, keepdims=True) + NORM_EPS)
            hn = hn * gain_ref[:, hh * ML_V:(hh + 1) * ML_V]
            og = og_ref[pl.ds(t0, L), hh * ML_V:(hh + 1) * ML_V].astype(F32)
            o_ref[pl.ds(t0, L), hh * ML_V:(hh + 1) * ML_V] = (hn * jax.nn.sigmoid(og)).astype(o_ref.dtype)
            m_new = jnp.sum(jnp.where(row1 == L - 1, m_t, 0.0), axis=0, keepdims=True)
            b_last = jnp.sum(jnp.where(col1 == L - 1, bc, 0.0), axis=-1, keepdims=True)
            decay = jnp.exp(b_last + m_prev - m_new)
            w_last = to_col(jnp.exp(b_last + a_row - m_new))
            kw = kp * w_last
            c_upd = lax.dot_general(kw.astype(BF16), vh, (((0,), (0,)), ((), ())), preferred_element_type=F32)
            c_ref[hh] = decay * c_state + c_upd
            n_ref[hh:hh + 1, :] = decay * n_state + jnp.sum(kw, axis=0, keepdims=True)
            m_ref[hh:hh + 1, :] = jnp.broadcast_to(m_new, (1, LANES))
        return carry

    lax.fori_loop(0, nc, chunk_step, 0)


def _mlstm(qk_conv, proj12_3d, v_blk0, og_blk0, rows, out_gain):
    b, s, _ = qk_conv.shape
    npair = ML_HEADS // 2
    return pl.pallas_call(
        functools.partial(_mlstm_kernel, chunk=ML_CHUNK),
        grid=(b, npair),
        in_specs=[pl.BlockSpec((None, s, LANES), lambda bi, hp: (bi, 0, hp)),
                  pl.BlockSpec((None, s, LANES), lambda bi, hp: (bi, 0, npair + hp)),
                  pl.BlockSpec((None, s, 2 * ML_V), lambda bi, hp: (bi, 0, v_blk0 + hp)),
                  pl.BlockSpec((None, s, 2 * ML_V), lambda bi, hp: (bi, 0, og_blk0 + hp)),
                  pl.BlockSpec((None, 24, s), lambda bi, hp: (bi, 0, 0)),
                  pl.BlockSpec((1, 2 * ML_V), lambda bi, hp: (0, hp))],
        out_specs=pl.BlockSpec((None, s, 2 * ML_V), lambda bi, hp: (bi, 0, hp)),
        out_shape=jax.ShapeDtypeStruct((b, s, ML_HEADS * ML_V), BF16),
        scratch_shapes=[pltpu.VMEM((2, LANES, ML_V), F32),
                        pltpu.VMEM((8, LANES), F32),
                        pltpu.VMEM((8, LANES), F32)],
        compiler_params=_params("parallel", "parallel"),
        name="mlstm",
    )(qk_conv, qk_conv, proj12_3d, proj12_3d, rows, out_gain.reshape(1, -1))


def _mla_prep_kernel(s_ref, cos_ref, sin_ref, wuq_ref, wukv_ref, cqg_ref, ckvg_ref, qgn_ref, qgr_ref,
                     kgn_ref, kgr_ref, q_ref, k_ref, v_ref):
    scale = MLA_QK ** -0.5
    lane = lax.broadcasted_iota(jnp.int32, (1, LANES), 1)
    cosf = cos_ref[...]
    sinf = sin_ref[...]

    def norm(x, g):
        return x * lax.rsqrt(jnp.mean(x * x, axis=-1, keepdims=True) + NORM_EPS) * g

    def rope(x):
        swap = jnp.where(lane < MLA_ROPE // 2, pltpu.roll(x, LANES - MLA_ROPE // 2, axis=1),
                         pltpu.roll(x, MLA_ROPE // 2, axis=1))
        return x * cosf + swap * sinf

    cq = s_ref[:, 0:MLA_Q_RANK]
    ckv = s_ref[:, MLA_Q_RANK:MLA_Q_RANK + MLA_KV_RANK]
    kr = jnp.where(lane < MLA_ROPE, s_ref[:, MLA_Q_RANK + MLA_KV_RANK:], 0.0)
    qf = jnp.dot(norm(cq, cqg_ref[...]).astype(BF16), wuq_ref[...], preferred_element_type=F32)
    kvf = jnp.dot(norm(ckv, ckvg_ref[...]).astype(BF16), wukv_ref[...], preferred_element_type=F32)
    ss_kr = jnp.sum(kr * kr, axis=-1, keepdims=True)
    kr_g = kr * kgr_ref[...]
    nw = MLA_HEADS * MLA_NOPE
    for h in range(MLA_HEADS):
        qn = qf[:, h * MLA_NOPE:(h + 1) * MLA_NOPE]
        qr = qf[:, nw + h * LANES:nw + (h + 1) * LANES]
        ss = jnp.sum(qn * qn, axis=-1, keepdims=True) + jnp.sum(qr * qr, axis=-1, keepdims=True)
        r = lax.rsqrt(ss / MLA_QK + NORM_EPS)
        q_ref[:, h * MLA_PAD:h * MLA_PAD + MLA_NOPE] = (qn * r * qgn_ref[...] * scale).astype(q_ref.dtype)
        q_ref[:, h * MLA_PAD + MLA_NOPE:(h + 1) * MLA_PAD] = (rope(qr * r * qgr_ref[...]) * scale).astype(q_ref.dtype)
        kn = kvf[:, h * 2 * MLA_NOPE:h * 2 * MLA_NOPE + MLA_NOPE]
        rk = lax.rsqrt((jnp.sum(kn * kn, axis=-1, keepdims=True) + ss_kr) / MLA_QK + NORM_EPS)
        k_ref[:, h * MLA_PAD:h * MLA_PAD + MLA_NOPE] = (kn * rk * kgn_ref[...]).astype(k_ref.dtype)
        k_ref[:, h * MLA_PAD + MLA_NOPE:(h + 1) * MLA_PAD] = rope(kr_g * rk).astype(k_ref.dtype)
        v_ref[:, h * MLA_V:(h + 1) * MLA_V] = kvf[:, h * 2 * MLA_NOPE + MLA_NOPE:(h + 1) * 2 * MLA_NOPE].astype(v_ref.dtype)


def _mla_prep(seg3, cosf, sinf, wuq_p, wukv_b, cq_gain, ckv_gain, q_gain, k_gain, tm=256):
    m, w3 = seg3.shape
    pad = jnp.zeros((LANES - MLA_ROPE,), F32)
    qgn = q_gain[:MLA_NOPE].reshape(1, -1)
    qgr = jnp.concatenate([q_gain[MLA_NOPE:], pad]).reshape(1, -1)
    kgn = k_gain[:MLA_NOPE].reshape(1, -1)
    kgr = jnp.concatenate([k_gain[MLA_NOPE:], pad]).reshape(1, -1)
    const = lambda i: (0, 0)
    return pl.pallas_call(
        _mla_prep_kernel,
        grid=(m // tm,),
        in_specs=[pl.BlockSpec((tm, w3), lambda i: (i, 0)),
                  pl.BlockSpec((tm, LANES), lambda i: (i, 0)),
                  pl.BlockSpec((tm, LANES), lambda i: (i, 0)),
                  pl.BlockSpec(wuq_p.shape, const),
                  pl.BlockSpec(wukv_b.shape, const),
                  pl.BlockSpec((1, MLA_Q_RANK), const),
                  pl.BlockSpec((1, MLA_KV_RANK), const),
                  pl.BlockSpec((1, LANES), const),
                  pl.BlockSpec((1, LANES), const),
                  pl.BlockSpec((1, LANES), const),
                  pl.BlockSpec((1, LANES), const)],
        out_specs=[pl.BlockSpec((tm, MLA_HEADS * MLA_PAD), lambda i: (i, 0)),
                   pl.BlockSpec((tm, MLA_HEADS * MLA_PAD), lambda i: (i, 0)),
                   pl.BlockSpec((tm, MLA_HEADS * MLA_V), lambda i: (i, 0))],
        out_shape=[jax.ShapeDtypeStruct((m, MLA_HEADS * MLA_PAD), BF16),
                   jax.ShapeDtypeStruct((m, MLA_HEADS * MLA_PAD), BF16),
                   jax.ShapeDtypeStruct((m, MLA_HEADS * MLA_V), BF16)],
        compiler_params=_params("parallel"),
        name="mla_prep",
    )(seg3, cosf, sinf, wuq_p, wukv_b, cq_gain.reshape(1, -1), ckv_gain.reshape(1, -1), qgn, qgr, kgn, kgr)


def _pack_w_in(w_in):
    fw = FOX_HEADS * FOX_DIM
    sizes = (fw, fw, fw, FOX_HEADS, ML_HEADS * ML_QK, ML_HEADS * ML_QK, ML_HEADS * ML_V,
             ML_HEADS, ML_HEADS, ML_HEADS * ML_V, MLA_Q_RANK, MLA_KV_RANK, MLA_ROPE)
    offs = [0]
    for sz in sizes:
        offs.append(offs[-1] + sz)
    seg = lambda i: w_in[:, offs[i]:offs[i + 1]]
    d = w_in.shape[0]
    w12 = jnp.concatenate([seg(0), seg(1), seg(2), seg(4), seg(5), seg(6), seg(9)], axis=1).astype(BF16)
    small = jnp.concatenate([seg(12), seg(3), seg(7), seg(8),
                             jnp.zeros((d, LANES - MLA_ROPE - FOX_HEADS - 2 * ML_HEADS), w_in.dtype)], axis=1)
    w3 = jnp.concatenate([seg(10), seg(11), small], axis=1).astype(BF16)
    wg = w_in[:, offs[-1]:].astype(BF16)
    return w12, w3, wg


def _layer(x, cosf, sinf, b, s, p, big, l):
    m, d = x.shape
    w12, w3, wg = _pack_w_in(p["w_in"])
    h1 = _rmsnorm(x, p["mix_norm"])

    proj12 = _ws_matmul([(h1, d, 0)], [(w12, d, 0, 0)], [], _ep_plain, tm=1024, tn=1024,
                        n_out=w12.shape[1], out_dtype=BF16, name="proj12")
    seg3 = _ws_matmul([(h1, d, 0)], [(w3, d, 0, 0)], [], _ep_plain, tm=1024, tn=512,
                      n_out=w3.shape[1], out_dtype=F32, name="proj3")
    gates = _ws_matmul([(h1, d, 0)], [(wg, d, 0, 0)], [(p["gate_bias"].reshape(1, -1), "row", 0)],
                       _ep_bias_sigmoid, tm=1024, tn=1024, n_out=wg.shape[1], out_dtype=BF16, name="gates")

    bias_vec = jnp.concatenate([jnp.zeros((MLA_ROPE,), F32), p["fox_f_bias"], p["mlstm_i_bias"], p["mlstm_f_bias"],
                                jnp.zeros((LANES - MLA_ROPE - 24,), F32)]).reshape(1, LANES)
    rows = _gate_rows(seg3, bias_vec, b, s)

    proj12_3d = proj12.reshape(b, s, -1)
    qk_fox = _fox_prep(proj12, p["fox_q_gain"], p["fox_k_gain"]).reshape(b, s, -1)
    o_fox = _flash(qk_fox, 0, qk_fox, FOX_HEADS, proj12_3d, 2 * FOX_HEADS, FOX_DIM, rows, FOX_HEADS,
                   FOX_HEADS * FOX_DIM).reshape(m, -1)
    qk_conv = _mlstm_conv(proj12_3d, 3, p["mlstm_conv_w"], p["mlstm_conv_b"])
    o_ml = _mlstm(qk_conv, proj12_3d, 4096 // (2 * ML_V), 5120 // (2 * ML_V), rows,
                  p["mlstm_out_gain"]).reshape(m, -1)
    wuq = p["mla_w_uq"].reshape(MLA_Q_RANK, MLA_HEADS, MLA_QK)
    wuq_p = jnp.concatenate([
        wuq[:, :, :MLA_NOPE].reshape(MLA_Q_RANK, -1),
        jnp.pad(wuq[:, :, MLA_NOPE:], ((0, 0), (0, 0), (0, LANES - MLA_ROPE))).reshape(MLA_Q_RANK, -1)],
        axis=1).astype(BF16)
    q_mla, k_mla, v_mla = _mla_prep(seg3, cosf, sinf, wuq_p, p["mla_w_ukv"].astype(BF16), p["mla_cq_gain"],
                                    p["mla_ckv_gain"], p["mla_q_gain"], p["mla_k_gain"])
    o_mla = _flash(q_mla.reshape(b, s, -1), 0, k_mla.reshape(b, s, -1), 0, v_mla.reshape(b, s, -1), 0,
                   MLA_PAD, None, MLA_HEADS, MLA_HEADS * MLA_V).reshape(m, -1)

    kb = o_fox.shape[1]
    nb = d // 512
    merged = _ws_matmul([(o_fox, kb, 0), (o_ml, kb, 0), (o_mla, kb, 0)],
                        [((big["w_fox_out"], l), kb, 0, 0), ((big["w_mlstm_out"], l), kb, 0, 0),
                         ((big["w_mla_out"], l), kb, 0, 0)],
                        [(gates, "tile", 0), (gates, "tile", nb), (gates, "tile", 2 * nb)],
                        _ep_merge, tm=1024, tn=512, n_out=d, out_dtype=BF16, name="merge")
    x = _ws_matmul([(merged, d, 0)], [((big["w_o"], l), d, 0, 0)], [(x, "tile", 0)], _ep_residual,
                   tm=1024, tn=512, n_out=d, out_dtype=F32, name="w_o")

    h2 = _rmsnorm(x, p["ffn_norm"])
    d_ff = big["w_gate"].shape[2]
    act = _ws_matmul([(h2, d, 0)], [((big["w_gate"], l), d, 0, 0), ((big["w_up"], l), d, 0, 0)], [], _ep_swiglu,
                     tm=1024, tn=256, n_out=d_ff, out_dtype=BF16, pairs=[(0, 0), (0, 1)], name="gate_up")
    kh = d_ff // 2
    for ki in range(2):
        x = _ws_matmul([(act, kh, ki)], [((big["w_down"], l), kh, ki, 0)], [(x, "tile", 0)], _ep_residual,
                       tm=512, tn=512, n_out=d, out_dtype=F32, name=f"down{ki}")
    return x


def kernel(x, positions, mix_norm, w_in, fox_f_bias, fox_q_gain, fox_k_gain, mlstm_conv_w, mlstm_conv_b,
           mlstm_i_bias, mlstm_f_bias, mlstm_out_gain, mla_cq_gain, mla_ckv_gain, mla_w_uq, mla_w_ukv,
           mla_q_gain, mla_k_gain, w_fox_out, w_mlstm_out, w_mla_out, gate_bias, w_o, ffn_norm, w_gate,
           w_up, w_down):
    b, s, d = x.shape
    depth = w_in.shape[0]
    inv_freq = jnp.power(ROPE_THETA, -jnp.arange(0, MLA_ROPE, 2, dtype=F32) / MLA_ROPE)
    ang = positions.astype(F32).reshape(b * s, 1) * inv_freq
    zpad = jnp.zeros((b * s, LANES - MLA_ROPE), F32)
    cosf = jnp.concatenate([jnp.cos(ang), jnp.cos(ang), zpad], axis=1)
    sinf = jnp.concatenate([-jnp.sin(ang), jnp.sin(ang), zpad], axis=1)
    names = dict(mix_norm=mix_norm, w_in=w_in, fox_f_bias=fox_f_bias, fox_q_gain=fox_q_gain, fox_k_gain=fox_k_gain,
                 mlstm_conv_w=mlstm_conv_w, mlstm_conv_b=mlstm_conv_b, mlstm_i_bias=mlstm_i_bias,
                 mlstm_f_bias=mlstm_f_bias, mlstm_out_gain=mlstm_out_gain, mla_cq_gain=mla_cq_gain,
                 mla_ckv_gain=mla_ckv_gain, mla_w_uq=mla_w_uq, mla_w_ukv=mla_w_ukv, mla_q_gain=mla_q_gain,
                 mla_k_gain=mla_k_gain, gate_bias=gate_bias, ffn_norm=ffn_norm)
    big = dict(w_fox_out=w_fox_out, w_mlstm_out=w_mlstm_out, w_mla_out=w_mla_out, w_o=w_o, w_gate=w_gate,
               w_up=w_up, w_down=w_down)
    xf = x.reshape(b * s, d)
    for l in range(depth):
        xf = _layer(xf, cosf, sinf, b, s, {k: v[l] for k, v in names.items()}, big, l)
    return xf.reshape(b, s, d)
```

```python
import functools

import jax
import jax.numpy as jnp
from jax import lax
from jax.experimental import pallas as pl
from jax.experimental.pallas import tpu as pltpu

F32 = jnp.float32
BF16 = jnp.bfloat16

NORM_EPS = 1e-6
ROPE_THETA = 10000.0

FOX_HEADS = 8
FOX_DIM = 128
ML_HEADS = 8
ML_QK = 64
ML_V = 128
ML_CONV = 4
MLA_HEADS = 8
MLA_Q_RANK = 896
MLA_KV_RANK = 512
MLA_NOPE = 128
MLA_ROPE = 64
MLA_V = 128
MLA_QK = MLA_NOPE + MLA_ROPE
MLA_PAD = 256

LANES = 128
VMEM_LIMIT = 56 * 1024 * 1024
ML_CHUNK = 256
ATT_BLOCK = 256
HALO = 16


def _params(*sem):
    return pltpu.CompilerParams(dimension_semantics=sem, vmem_limit_bytes=VMEM_LIMIT)


def _rmsnorm_kernel(x_ref, g_ref, o_ref):
    x = x_ref[...]
    ms = jnp.mean(x * x, axis=-1, keepdims=True)
    o_ref[...] = (x * lax.rsqrt(ms + NORM_EPS) * g_ref[...]).astype(o_ref.dtype)


def _rmsnorm(x, gain, tm=256):
    m, d = x.shape
    return pl.pallas_call(
        _rmsnorm_kernel,
        grid=(m // tm,),
        in_specs=[pl.BlockSpec((tm, d), lambda i: (i, 0)),
                  pl.BlockSpec((1, d), lambda i: (0, 0))],
        out_specs=pl.BlockSpec((tm, d), lambda i: (i, 0)),
        out_shape=jax.ShapeDtypeStruct((m, d), BF16),
        compiler_params=_params("parallel"),
        name="rmsnorm",
    )(x, gain.reshape(1, d))


def _ws_kernel(*refs, nx, nw, ne, pairs, cast, epilogue, cast_rows, shift):
    x_refs = refs[:nx]
    w_refs = refs[nx:nx + nw]
    e_refs = refs[nx + nw:nx + nw + ne]
    o_ref = refs[nx + nw + ne]
    s_refs = refs[nx + nw + ne + 1:]
    if cast:
        @pl.when(pl.program_id(1) == 0)
        def _():
            for w_ref, s_ref in zip(w_refs, s_refs):
                k, tn = s_ref.shape

                def body(i, c, w_ref=w_ref, s_ref=s_ref, tn=tn):
                    r = pl.multiple_of(i * cast_rows, cast_rows)
                    s_ref[pl.ds(r, cast_rows), :] = w_ref[pl.ds(r, cast_rows), shift:shift + tn].astype(BF16)
                    return c

                lax.fori_loop(0, k // cast_rows, body, 0)
        w_src = s_refs
    else:
        w_src = w_refs
    dots = [jnp.dot(x_refs[xi][...], w_src[wi][...], preferred_element_type=F32) for xi, wi in pairs]
    o_ref[...] = epilogue(dots, [e[...] for e in e_refs]).astype(o_ref.dtype)


def _ws_matmul(xs, ws, extras, epilogue, *, tm, tn, n_out, out_dtype, pairs=None, first_col=None, name):
    m = xs[0][0].shape[0]
    pairs = pairs or [(i, i) for i in range(len(ws))]
    first_w = ws[0][0][0] if isinstance(ws[0][0], tuple) else ws[0][0]
    cast = first_w.dtype != BF16
    in_specs = []
    args = []
    shift = 0
    for a, kb, ki in xs:
        in_specs.append(pl.BlockSpec((tm, kb), lambda n, i, ki=ki: (i, ki)))
        args.append(a)
    for a, kb, ki, off in ws:
        if first_col is not None:
            a, layer = a
            assert cast and len(ws) == 1 and kb == a.shape[1] and tn % LANES == 0
            shift = first_col % LANES
            base_blk = first_col // LANES
            width = tn + LANES
            over = base_blk * LANES + (pl.cdiv(n_out, tn) - 1) * tn + width - a.shape[2]
            in_specs.append(pl.BlockSpec(
                (None, pl.Element(kb), pl.Element(width, padding=(0, max(over, 0)))),
                lambda n, i, layer=layer, base_blk=base_blk: (layer, 0, (base_blk + n * (tn // LANES)) * LANES)))
        elif isinstance(a, tuple):
            a, layer = a
            in_specs.append(pl.BlockSpec((None, kb, tn), lambda n, i, ki=ki, off=off, layer=layer: (layer, ki, n + off)))
        else:
            in_specs.append(pl.BlockSpec((kb, tn), lambda n, i, ki=ki, off=off: (ki, n + off)))
        args.append(a)
    for a, kind, off in extras:
        if kind == "tile":
            in_specs.append(pl.BlockSpec((tm, tn), lambda n, i, off=off: (i, n + off)))
        else:
            in_specs.append(pl.BlockSpec((1, tn), lambda n, i, off=off: (0, n + off)))
        args.append(a)
    scratch = [pltpu.VMEM((kb, tn), BF16) for _, kb, _, _ in ws] if cast else []
    cast_rows = LANES
    for _, kb, _, _ in ws:
        assert kb % cast_rows == 0 or not cast, kb
    kern = functools.partial(_ws_kernel, nx=len(xs), nw=len(ws), ne=len(extras), pairs=pairs, cast=cast,
                             epilogue=epilogue, cast_rows=cast_rows, shift=shift)
    return pl.pallas_call(
        kern,
        grid=(pl.cdiv(n_out, tn), m // tm),
        in_specs=in_specs,
        out_specs=pl.BlockSpec((tm, tn), lambda n, i: (i, n)),
        out_shape=jax.ShapeDtypeStruct((m, n_out), out_dtype),
        scratch_shapes=scratch,
        compiler_params=_params("parallel", "arbitrary"),
        name=name,
    )(*args)


def _ep_plain(dots, extras):
    return dots[0]


def _ep_bias_sigmoid(dots, extras):
    return jax.nn.sigmoid(dots[0] + extras[0])


def _ep_residual(dots, extras):
    return extras[0] + dots[0]


def _ep_swiglu(dots, extras):
    g, u = dots
    return g * jax.nn.sigmoid(g) * u


def _ep_merge(dots, extras):
    return (extras[0].astype(F32) * dots[0] + extras[1].astype(F32) * dots[1]
            + extras[2].astype(F32) * dots[2])


def _log_sigmoid(x):
    return jnp.minimum(x, 0.0) - jnp.log1p(jnp.exp(-jnp.abs(x)))


def _cumsum_lanes(x, seg):
    n = x.shape[-1]
    pos = lax.broadcasted_iota(jnp.int32, x.shape, 1) % seg
    shift = 1
    while shift < seg:
        x = x + jnp.where(pos >= shift, pltpu.roll(x, shift, axis=1), 0.0)
        shift *= 2
    del n
    return x


def _gates_kernel(g_ref, bias_ref, rows_ref, *, chunk):
    g = g_ref[...] + bias_ref[...]
    gt = g.T
    s = gt.shape[1]
    rows_ref[0:8, :] = _cumsum_lanes(_log_sigmoid(gt[64:72, :]), s)
    rows_ref[8:16, :] = gt[72:80, :]
    rows_ref[16:24, :] = _cumsum_lanes(_log_sigmoid(gt[80:88, :]), chunk)


def _gate_rows(g_pre, bias_vec, b, s):
    return pl.pallas_call(
        functools.partial(_gates_kernel, chunk=ML_CHUNK),
        grid=(b,),
        in_specs=[pl.BlockSpec((s, LANES), lambda i: (i, 0)),
                  pl.BlockSpec((1, LANES), lambda i: (0, 0))],
        out_specs=pl.BlockSpec((None, 24, s), lambda i: (i, 0, 0)),
        out_shape=jax.ShapeDtypeStruct((b, 24, s), F32),
        compiler_params=_params("parallel"),
        name="gate_rows",
    )(g_pre, bias_vec)


def _fox_prep_kernel(x_ref, qg_ref, kg_ref, o_ref):
    scale = FOX_DIM ** -0.5
    for h in range(2 * FOX_HEADS):
        x = x_ref[:, h * FOX_DIM:(h + 1) * FOX_DIM].astype(F32)
        y = x * lax.rsqrt(jnp.mean(x * x, axis=-1, keepdims=True) + NORM_EPS)
        if h < FOX_HEADS:
            y = y * qg_ref[...] * scale
        else:
            y = y * kg_ref[...]
        o_ref[:, h * FOX_DIM:(h + 1) * FOX_DIM] = y.astype(o_ref.dtype)


def _fox_prep(proj12, q_gain, k_gain, tm=512):
    m = proj12.shape[0]
    w = 2 * FOX_HEADS * FOX_DIM
    return pl.pallas_call(
        _fox_prep_kernel,
        grid=(m // tm,),
        in_specs=[pl.BlockSpec((tm, w), lambda i: (i, 0)),
                  pl.BlockSpec((1, FOX_DIM), lambda i: (0, 0)),
                  pl.BlockSpec((1, FOX_DIM), lambda i: (0, 0))],
        out_specs=pl.BlockSpec((tm, w), lambda i: (i, 0)),
        out_shape=jax.ShapeDtypeStruct((m, w), BF16),
        compiler_params=_params("parallel"),
        name="fox_prep",
    )(proj12, q_gain.reshape(1, -1), k_gain.reshape(1, -1))


def _flash_kernel(*refs, blk, has_bias):
    if has_bias:
        q_ref, k_ref, v_ref, d_ref, o_ref = refs
    else:
        q_ref, k_ref, v_ref, o_ref = refs
        d_ref = None
    h = pl.program_id(1)
    s_len = k_ref.shape[0]
    row = lax.broadcasted_iota(jnp.int32, (blk, blk), 0)
    col = lax.broadcasted_iota(jnp.int32, (blk, blk), 1)
    nt = (((1,), (1,)), ((), ()))
    for qi in range(s_len // blk):
        lo = qi * blk
        q = q_ref[lo:lo + blk, :]
        s_diag = lax.dot_general(q, k_ref[lo:lo + blk, :], nt, preferred_element_type=F32)
        if has_bias:
            s_diag = s_diag - d_ref[pl.ds(h, 1), lo:lo + blk]
        s_diag = jnp.where(col <= row, s_diag, -jnp.inf)
        m = jnp.max(s_diag, axis=-1, keepdims=True)
        if qi:
            s_off = lax.dot_general(q, k_ref[0:lo, :], nt, preferred_element_type=F32)
            if has_bias:
                s_off = s_off - d_ref[pl.ds(h, 1), 0:lo]
            m = jnp.maximum(m, jnp.max(s_off, axis=-1, keepdims=True))
        p_diag = jnp.exp(s_diag - m)
        l = jnp.sum(p_diag, axis=-1, keepdims=True)
        acc = jnp.dot(p_diag.astype(BF16), v_ref[lo:lo + blk, :], preferred_element_type=F32)
        if qi:
            p_off = jnp.exp(s_off - m)
            l = l + jnp.sum(p_off, axis=-1, keepdims=True)
            acc = acc + jnp.dot(p_off.astype(BF16), v_ref[0:lo, :], preferred_element_type=F32)
        o_ref[lo:lo + blk, :] = (acc / l).astype(o_ref.dtype)


def _flash(q_arr, q_off, k_arr, k_off, v_arr, v_off, dk, rows, n_heads, out_width):
    b, s, _ = q_arr.shape
    blk = ATT_BLOCK
    in_specs = [
        pl.BlockSpec((None, s, dk), lambda bi, h: (bi, 0, q_off + h)),
        pl.BlockSpec((None, s, dk), lambda bi, h: (bi, 0, k_off + h)),
        pl.BlockSpec((None, s, LANES), lambda bi, h: (bi, 0, v_off + h)),
    ]
    args = [q_arr, k_arr, v_arr]
    if rows is not None:
        in_specs.append(pl.BlockSpec((None, 8, s), lambda bi, h: (bi, 0, 0)))
        args.append(rows)
    return pl.pallas_call(
        functools.partial(_flash_kernel, blk=blk, has_bias=rows is not None),
        grid=(b, n_heads),
        in_specs=in_specs,
        out_specs=pl.BlockSpec((None, s, LANES), lambda bi, h: (bi, 0, h)),
        out_shape=jax.ShapeDtypeStruct((b, s, out_width), BF16),
        compiler_params=_params("parallel", "parallel"),
        name="flash_bias" if rows is not None else "flash",
    )(*args)


def _conv_kernel(x_ref, halo_ref, w_ref, b_ref, o_ref, xs_ref):
    tm = x_ref.shape[0]
    first = pl.program_id(1) == 0
    halo = halo_ref[...].astype(F32)
    xs_ref[0:HALO, :] = jnp.where(first, 0.0, halo)
    xs_ref[HALO:, :] = x_ref[...].astype(F32)
    y = b_ref[...]
    for j in range(ML_CONV):
        off = HALO - (ML_CONV - 1) + j
        y = y + w_ref[j:j + 1, :] * xs_ref[off:off + tm, :]
    o_ref[...] = (y * jax.nn.sigmoid(y)).astype(o_ref.dtype)


def _mlstm_conv(proj12_3d, col_block, conv_w, conv_b, tm=256):
    b, s, _ = proj12_3d.shape
    c = conv_w.shape[1]
    hb = tm // HALO
    return pl.pallas_call(
        _conv_kernel,
        grid=(b, s // tm),
        in_specs=[pl.BlockSpec((None, tm, c), lambda bi, i: (bi, i, col_block)),
                  pl.BlockSpec((None, HALO, c), lambda bi, i: (bi, jnp.maximum(i * hb - 1, 0), col_block)),
                  pl.BlockSpec((ML_CONV, c), lambda bi, i: (0, 0)),
                  pl.BlockSpec((1, c), lambda bi, i: (0, 0))],
        out_specs=pl.BlockSpec((None, tm, c), lambda bi, i: (bi, i, 0)),
        out_shape=jax.ShapeDtypeStruct((b, s, c), BF16),
        scratch_shapes=[pltpu.VMEM((tm + HALO, c), F32)],
        compiler_params=_params("parallel", "parallel"),
        name="mlstm_conv",
    )(proj12_3d, proj12_3d, conv_w, conv_b.reshape(1, c))


def _mlstm_kernel(q_ref, k_ref, v_ref, og_ref, rows_ref, gain_ref, o_ref, c_ref, n_ref, m_ref, *, chunk):
    L = chunk
    hp = pl.program_id(1)
    s = q_ref.shape[0]
    nc = s // L
    c_ref[...] = jnp.zeros_like(c_ref)
    n_ref[...] = jnp.zeros_like(n_ref)
    m_ref[...] = jnp.zeros_like(m_ref)

    lane = lax.broadcasted_iota(jnp.int32, (1, LANES), 1)
    row = lax.broadcasted_iota(jnp.int32, (L, L), 0)
    col = lax.broadcasted_iota(jnp.int32, (L, L), 1)
    causal = col <= row
    eye = col == row
    col1 = lax.broadcasted_iota(jnp.int32, (1, L), 1)
    row1 = lax.broadcasted_iota(jnp.int32, (L, 1), 0)

    def to_col(r):
        return jnp.sum(jnp.where(eye, r, 0.0), axis=-1, keepdims=True)

    def chunk_step(ci, carry):
        t0 = pl.multiple_of(ci * L, L)
        qp = q_ref[pl.ds(t0, L), :]
        kp = k_ref[pl.ds(t0, L), :].astype(F32) * (ML_QK ** -0.5)
        kp_b = kp.astype(BF16)
        for hh in range(2):
            head = 2 * hp + hh
            li = rows_ref[pl.ds(8 + head, 1), pl.ds(t0, L)]
            bc = rows_ref[pl.ds(16 + head, 1), pl.ds(t0, L)]
            m_prev = m_ref[hh:hh + 1, 0:1]
            a_row = li - bc
            bc_col = to_col(bc)
            log_w = jnp.where(causal, bc_col + a_row, -jnp.inf)
            log_inter = bc_col + m_prev
            m_t = jnp.maximum(log_inter, jnp.max(log_w, axis=-1, keepdims=True))
            w = jnp.exp(log_w - m_t)
            inter = jnp.exp(log_inter - m_t)
            head_lanes = (lane >= hh * ML_QK) & (lane < (hh + 1) * ML_QK)
            qh = jnp.where(head_lanes, qp, jnp.zeros_like(qp))
            qk = lax.dot_general(qh, kp_b, (((1,), (1,)), ((), ())), preferred_element_type=F32) * w
            vh = v_ref[pl.ds(t0, L), hh * ML_V:(hh + 1) * ML_V]
            c_state = c_ref[hh]
            n_state = n_ref[hh:hh + 1, :]
            num = (jnp.dot(qk.astype(BF16), vh, preferred_element_type=F32)
                   + inter * jnp.dot(qh, c_state.astype(BF16), preferred_element_type=F32))
            den = (jnp.sum(qk, axis=-1, keepdims=True)
                   + inter * jnp.sum(qh.astype(F32) * n_state, axis=-1, keepdims=True))
            h_out = num / jnp.maximum(jnp.abs(den), jnp.exp(-m_t))
            hn = h_out * lax.rsqrt(jnp.mean(h_out * h_out, axis=-1, keepdims=True) + NORM_EPS)
            hn = hn * gain_ref[:, hh * ML_V:(hh + 1) * ML_V]
            og = og_ref[pl.ds(t0, L), hh * ML_V:(hh + 1) * ML_V].astype(F32)
            o_ref[pl.ds(t0, L), hh * ML_V:(hh + 1) * ML_V] = (hn * jax.nn.sigmoid(og)).astype(o_ref.dtype)
            m_new = jnp.sum(jnp.where(row1 == L - 1, m_t, 0.0), axis=0, keepdims=True)
            b_last = jnp.sum(jnp.where(col1 == L - 1, bc, 0.0), axis=-1, keepdims=True)
            decay = jnp.exp(b_last + m_prev - m_new)
            w_last = to_col(jnp.exp(b_last + a_row - m_new))
            kw = kp * w_last
            c_upd = lax.dot_general(kw.astype(BF16), vh, (((0,), (0,)), ((), ())), preferred_element_type=F32)
            c_ref[hh] = decay * c_state + c_upd
            n_ref[hh:hh + 1, :] = decay * n_state + jnp.sum(kw, axis=0, keepdims=True)
            m_ref[hh:hh + 1, :] = jnp.broadcast_to(m_new, (1, LANES))
        return carry

    lax.fori_loop(0, nc, chunk_step, 0)


def _mlstm(qk_conv, v_arr, v_blk0, og_arr, og_blk0, rows, out_gain):
    b, s, _ = qk_conv.shape
    npair = ML_HEADS // 2
    return pl.pallas_call(
        functools.partial(_mlstm_kernel, chunk=ML_CHUNK),
        grid=(b, npair),
        in_specs=[pl.BlockSpec((None, s, LANES), lambda bi, hp: (bi, 0, hp)),
                  pl.BlockSpec((None, s, LANES), lambda bi, hp: (bi, 0, npair + hp)),
                  pl.BlockSpec((None, s, 2 * ML_V), lambda bi, hp: (bi, 0, v_blk0 + hp)),
                  pl.BlockSpec((None, s, 2 * ML_V), lambda bi, hp: (bi, 0, og_blk0 + hp)),
                  pl.BlockSpec((None, 24, s), lambda bi, hp: (bi, 0, 0)),
                  pl.BlockSpec((1, 2 * ML_V), lambda bi, hp: (0, hp))],
        out_specs=pl.BlockSpec((None, s, 2 * ML_V), lambda bi, hp: (bi, 0, hp)),
        out_shape=jax.ShapeDtypeStruct((b, s, ML_HEADS * ML_V), BF16),
        scratch_shapes=[pltpu.VMEM((2, LANES, ML_V), F32),
                        pltpu.VMEM((8, LANES), F32),
                        pltpu.VMEM((8, LANES), F32)],
        compiler_params=_params("parallel", "parallel"),
        name="mlstm",
    )(qk_conv, qk_conv, v_arr, og_arr, rows, out_gain.reshape(1, -1))


def _mla_prep_kernel(s_ref, cos_ref, sin_ref, wuq_ref, wukv_ref, cqg_ref, ckvg_ref, qgn_ref, qgr_ref,
                     kgn_ref, kgr_ref, q_ref, k_ref, v_ref):
    scale = MLA_QK ** -0.5
    lane = lax.broadcasted_iota(jnp.int32, (1, LANES), 1)
    cosf = cos_ref[...]
    sinf = sin_ref[...]

    def norm(x, g):
        return x * lax.rsqrt(jnp.mean(x * x, axis=-1, keepdims=True) + NORM_EPS) * g

    def rope(x):
        swap = jnp.where(lane < MLA_ROPE // 2, pltpu.roll(x, LANES - MLA_ROPE // 2, axis=1),
                         pltpu.roll(x, MLA_ROPE // 2, axis=1))
        return x * cosf + swap * sinf

    cq = s_ref[:, 0:MLA_Q_RANK]
    ckv = s_ref[:, MLA_Q_RANK:MLA_Q_RANK + MLA_KV_RANK]
    kr = jnp.where(lane < MLA_ROPE, s_ref[:, MLA_Q_RANK + MLA_KV_RANK:], 0.0)
    qf = jnp.dot(norm(cq, cqg_ref[...]).astype(BF16), wuq_ref[...], preferred_element_type=F32)
    kvf = jnp.dot(norm(ckv, ckvg_ref[...]).astype(BF16), wukv_ref[...], preferred_element_type=F32)
    ss_kr = jnp.sum(kr * kr, axis=-1, keepdims=True)
    kr_g = kr * kgr_ref[...]
    nw = MLA_HEADS * MLA_NOPE
    for h in range(MLA_HEADS):
        qn = qf[:, h * MLA_NOPE:(h + 1) * MLA_NOPE]
        qr = qf[:, nw + h * LANES:nw + (h + 1) * LANES]
        ss = jnp.sum(qn * qn, axis=-1, keepdims=True) + jnp.sum(qr * qr, axis=-1, keepdims=True)
        r = lax.rsqrt(ss / MLA_QK + NORM_EPS)
        q_ref[:, h * MLA_PAD:h * MLA_PAD + MLA_NOPE] = (qn * r * qgn_ref[...] * scale).astype(q_ref.dtype)
        q_ref[:, h * MLA_PAD + MLA_NOPE:(h + 1) * MLA_PAD] = (rope(qr * r * qgr_ref[...]) * scale).astype(q_ref.dtype)
        kn = kvf[:, h * 2 * MLA_NOPE:h * 2 * MLA_NOPE + MLA_NOPE]
        rk = lax.rsqrt((jnp.sum(kn * kn, axis=-1, keepdims=True) + ss_kr) / MLA_QK + NORM_EPS)
        k_ref[:, h * MLA_PAD:h * MLA_PAD + MLA_NOPE] = (kn * rk * kgn_ref[...]).astype(k_ref.dtype)
        k_ref[:, h * MLA_PAD + MLA_NOPE:(h + 1) * MLA_PAD] = rope(kr_g * rk).astype(k_ref.dtype)
        v_ref[:, h * MLA_V:(h + 1) * MLA_V] = kvf[:, h * 2 * MLA_NOPE + MLA_NOPE:(h + 1) * 2 * MLA_NOPE].astype(v_ref.dtype)


def _mla_prep(seg3, cosf, sinf, wuq_p, wukv_b, cq_gain, ckv_gain, q_gain, k_gain, tm=256):
    m, w3 = seg3.shape
    pad = jnp.zeros((LANES - MLA_ROPE,), F32)
    qgn = q_gain[:MLA_NOPE].reshape(1, -1)
    qgr = jnp.concatenate([q_gain[MLA_NOPE:], pad]).reshape(1, -1)
    kgn = k_gain[:MLA_NOPE].reshape(1, -1)
    kgr = jnp.concatenate([k_gain[MLA_NOPE:], pad]).reshape(1, -1)
    const = lambda i: (0, 0)
    return pl.pallas_call(
        _mla_prep_kernel,
        grid=(m // tm,),
        in_specs=[pl.BlockSpec((tm, w3), lambda i: (i, 0)),
                  pl.BlockSpec((tm, LANES), lambda i: (i, 0)),
                  pl.BlockSpec((tm, LANES), lambda i: (i, 0)),
                  pl.BlockSpec(wuq_p.shape, const),
                  pl.BlockSpec(wukv_b.shape, const),
                  pl.BlockSpec((1, MLA_Q_RANK), const),
                  pl.BlockSpec((1, MLA_KV_RANK), const),
                  pl.BlockSpec((1, LANES), const),
                  pl.BlockSpec((1, LANES), const),
                  pl.BlockSpec((1, LANES), const),
                  pl.BlockSpec((1, LANES), const)],
        out_specs=[pl.BlockSpec((tm, MLA_HEADS * MLA_PAD), lambda i: (i, 0)),
                   pl.BlockSpec((tm, MLA_HEADS * MLA_PAD), lambda i: (i, 0)),
                   pl.BlockSpec((tm, MLA_HEADS * MLA_V), lambda i: (i, 0))],
        out_shape=[jax.ShapeDtypeStruct((m, MLA_HEADS * MLA_PAD), BF16),
                   jax.ShapeDtypeStruct((m, MLA_HEADS * MLA_PAD), BF16),
                   jax.ShapeDtypeStruct((m, MLA_HEADS * MLA_V), BF16)],
        compiler_params=_params("parallel"),
        name="mla_prep",
    )(seg3, cosf, sinf, wuq_p, wukv_b, cq_gain.reshape(1, -1), ckv_gain.reshape(1, -1), qgn, qgr, kgn, kgr)


_FW = FOX_HEADS * FOX_DIM
_IN_SIZES = (_FW, _FW, _FW, FOX_HEADS, ML_HEADS * ML_QK, ML_HEADS * ML_QK, ML_HEADS * ML_V,
             ML_HEADS, ML_HEADS, ML_HEADS * ML_V, MLA_Q_RANK, MLA_KV_RANK, MLA_ROPE)
_IN_OFFS = [sum(_IN_SIZES[:i]) for i in range(len(_IN_SIZES) + 1)]
(_C_FQ, _, _, _C_FF, _C_MQ, _, _, _C_MI, _C_MF, _C_MO, _C_CQ, _, _, _C_GATES) = _IN_OFFS


def _small_gate_weight(w_in, l):
    d = w_in.shape[1]
    cols = [jnp.zeros((d, MLA_ROPE), w_in.dtype),
            w_in[l, :, _C_FF:_C_FF + FOX_HEADS],
            w_in[l, :, _C_MI:_C_MI + ML_HEADS],
            w_in[l, :, _C_MF:_C_MF + ML_HEADS],
            jnp.zeros((d, LANES - MLA_ROPE - FOX_HEADS - 2 * ML_HEADS), w_in.dtype)]
    return jnp.concatenate(cols, axis=1).astype(BF16)


def _layer(x, cosf, sinf, b, s, p, big, l):
    m, d = x.shape
    h1 = _rmsnorm(x, p["mix_norm"])
    w_in = (big["w_in"], l)

    def in_proj(first_col, width, out_dtype, name, extras=(), epilogue=_ep_plain):
        return _ws_matmul([(h1, d, 0)], [(w_in, d, 0, 0)], list(extras), epilogue, tm=1024, tn=512, n_out=width,
                          out_dtype=out_dtype, first_col=first_col, name=name)

    proj_fox = in_proj(_C_FQ, 3 * _FW, BF16, "proj_fox")
    proj_ml = in_proj(_C_MQ, 2 * ML_HEADS * ML_QK + ML_HEADS * ML_V, BF16, "proj_ml")
    proj_mo = in_proj(_C_MO, ML_HEADS * ML_V, BF16, "proj_mo")
    seg3 = in_proj(_C_CQ, MLA_Q_RANK + MLA_KV_RANK + LANES, F32, "proj_mla")
    gates = in_proj(_C_GATES, 3 * d, BF16, "gates", [(p["gate_bias"].reshape(1, -1), "row", 0)], _ep_bias_sigmoid)
    g_pre = _ws_matmul([(h1, d, 0)], [(_small_gate_weight(big["w_in"], l), d, 0, 0)], [], _ep_plain, tm=1024,
                       tn=LANES, n_out=LANES, out_dtype=F32, name="proj_gate")

    bias_vec = jnp.concatenate([jnp.zeros((MLA_ROPE,), F32), p["fox_f_bias"], p["mlstm_i_bias"], p["mlstm_f_bias"],
                                jnp.zeros((LANES - MLA_ROPE - 24,), F32)]).reshape(1, LANES)
    rows = _gate_rows(g_pre, bias_vec, b, s)

    qk_fox = _fox_prep(proj_fox, p["fox_q_gain"], p["fox_k_gain"]).reshape(b, s, -1)
    o_fox = _flash(qk_fox, 0, qk_fox, FOX_HEADS, proj_fox.reshape(b, s, -1), 2 * FOX_HEADS, FOX_DIM, rows, FOX_HEADS,
                   FOX_HEADS * FOX_DIM).reshape(m, -1)
    proj_ml_3d = proj_ml.reshape(b, s, -1)
    qk_conv = _mlstm_conv(proj_ml_3d, 0, p["mlstm_conv_w"], p["mlstm_conv_b"])
    o_ml = _mlstm(qk_conv, proj_ml_3d, 2 * ML_HEADS * ML_QK // (2 * ML_V), proj_mo.reshape(b, s, -1), 0, rows,
                  p["mlstm_out_gain"]).reshape(m, -1)
    wuq = p["mla_w_uq"].reshape(MLA_Q_RANK, MLA_HEADS, MLA_QK)
    wuq_p = jnp.concatenate([
        wuq[:, :, :MLA_NOPE].reshape(MLA_Q_RANK, -1),
        jnp.pad(wuq[:, :, MLA_NOPE:], ((0, 0), (0, 0), (0, LANES - MLA_ROPE))).reshape(MLA_Q_RANK, -1)],
        axis=1).astype(BF16)
    q_mla, k_mla, v_mla = _mla_prep(seg3, cosf, sinf, wuq_p, p["mla_w_ukv"].astype(BF16), p["mla_cq_gain"],
                                    p["mla_ckv_gain"], p["mla_q_gain"], p["mla_k_gain"])
    o_mla = _flash(q_mla.reshape(b, s, -1), 0, k_mla.reshape(b, s, -1), 0, v_mla.reshape(b, s, -1), 0,
                   MLA_PAD, None, MLA_HEADS, MLA_HEADS * MLA_V).reshape(m, -1)

    kb = o_fox.shape[1]
    nb = d // 512
    merged = _ws_matmul([(o_fox, kb, 0), (o_ml, kb, 0), (o_mla, kb, 0)],
                        [((big["w_fox_out"], l), kb, 0, 0), ((big["w_mlstm_out"], l), kb, 0, 0),
                         ((big["w_mla_out"], l), kb, 0, 0)],
                        [(gates, "tile", 0), (gates, "tile", nb), (gates, "tile", 2 * nb)],
                        _ep_merge, tm=1024, tn=512, n_out=d, out_dtype=BF16, name="merge")
    x = _ws_matmul([(merged, d, 0)], [((big["w_o"], l), d, 0, 0)], [(x, "tile", 0)], _ep_residual,
                   tm=1024, tn=512, n_out=d, out_dtype=F32, name="w_o")

    h2 = _rmsnorm(x, p["ffn_norm"])
    d_ff = big["w_gate"].shape[2]
    act = _ws_matmul([(h2, d, 0)], [((big["w_gate"], l), d, 0, 0), ((big["w_up"], l), d, 0, 0)], [], _ep_swiglu,
                     tm=1024, tn=256, n_out=d_ff, out_dtype=BF16, pairs=[(0, 0), (0, 1)], name="gate_up")
    kh = d_ff // 2
    for ki in range(2):
        x = _ws_matmul([(act, kh, ki)], [((big["w_down"], l), kh, ki, 0)], [(x, "tile", 0)], _ep_residual,
                       tm=512, tn=512, n_out=d, out_dtype=F32, name=f"down{ki}")
    return x


def kernel(x, positions, mix_norm, w_in, fox_f_bias, fox_q_gain, fox_k_gain, mlstm_conv_w, mlstm_conv_b,
           mlstm_i_bias, mlstm_f_bias, mlstm_out_gain, mla_cq_gain, mla_ckv_gain, mla_w_uq, mla_w_ukv,
           mla_q_gain, mla_k_gain, w_fox_out, w_mlstm_out, w_mla_out, gate_bias, w_o, ffn_norm, w_gate,
           w_up, w_down):
    b, s, d = x.shape
    depth = w_in.shape[0]
    inv_freq = jnp.power(ROPE_THETA, -jnp.arange(0, MLA_ROPE, 2, dtype=F32) / MLA_ROPE)
    ang = positions.astype(F32).reshape(b * s, 1) * inv_freq
    zpad = jnp.zeros((b * s, LANES - MLA_ROPE), F32)
    cosf = jnp.concatenate([jnp.cos(ang), jnp.cos(ang), zpad], axis=1)
    sinf = jnp.concatenate([-jnp.sin(ang), jnp.sin(ang), zpad], axis=1)
    names = dict(mix_norm=mix_norm, fox_f_bias=fox_f_bias, fox_q_gain=fox_q_gain, fox_k_gain=fox_k_gain,
                 mlstm_conv_w=mlstm_conv_w, mlstm_conv_b=mlstm_conv_b, mlstm_i_bias=mlstm_i_bias,
                 mlstm_f_bias=mlstm_f_bias, mlstm_out_gain=mlstm_out_gain, mla_cq_gain=mla_cq_gain,
                 mla_ckv_gain=mla_ckv_gain, mla_w_uq=mla_w_uq, mla_w_ukv=mla_w_ukv, mla_q_gain=mla_q_gain,
                 mla_k_gain=mla_k_gain, gate_bias=gate_bias, ffn_norm=ffn_norm)
    big = dict(w_in=w_in, w_fox_out=w_fox_out, w_mlstm_out=w_mlstm_out, w_mla_out=w_mla_out, w_o=w_o, w_gate=w_gate,
               w_up=w_up, w_down=w_down)
    xf = x.reshape(b * s, d)
    for l in range(depth):
        xf = _layer(xf, cosf, sinf, b, s, {k: v[l] for k, v in names.items()}, big, l)
    return xf.reshape(b, s, d)
```

```python
import functools

import jax
import jax.numpy as jnp
from jax import lax
from jax.experimental import pallas as pl
from jax.experimental.pallas import tpu as pltpu

F32 = jnp.float32
BF16 = jnp.bfloat16

NORM_EPS = 1e-6
ROPE_THETA = 10000.0

FOX_HEADS = 8
FOX_DIM = 128
ML_HEADS = 8
ML_QK = 64
ML_V = 128
ML_CONV = 4
MLA_HEADS = 8
MLA_Q_RANK = 896
MLA_KV_RANK = 512
MLA_NOPE = 128
MLA_ROPE = 64
MLA_V = 128
MLA_QK = MLA_NOPE + MLA_ROPE
MLA_PAD = 256

LANES = 128
SUBLANES = 8
VMEM_LIMIT = 56 * 1024 * 1024
ML_CHUNK = 256
ATT_BLOCK = 256
HALO = 16


def _params(*sem):
    return pltpu.CompilerParams(dimension_semantics=sem, vmem_limit_bytes=VMEM_LIMIT)


def _rmsnorm_kernel(x_ref, g_ref, o_ref):
    x = x_ref[...]
    ms = jnp.mean(x * x, axis=-1, keepdims=True)
    o_ref[...] = (x * lax.rsqrt(ms + NORM_EPS) * g_ref[...]).astype(o_ref.dtype)


def _rmsnorm(x, gain, tm=256):
    m, d = x.shape
    return pl.pallas_call(
        _rmsnorm_kernel,
        grid=(m // tm,),
        in_specs=[pl.BlockSpec((tm, d), lambda i: (i, 0)),
                  pl.BlockSpec((1, d), lambda i: (0, 0))],
        out_specs=pl.BlockSpec((tm, d), lambda i: (i, 0)),
        out_shape=jax.ShapeDtypeStruct((m, d), BF16),
        compiler_params=_params("parallel"),
        name="rmsnorm",
    )(x, gain.reshape(1, d))


def _ws_kernel(*refs, nx, nw, ne, pairs, cast, epilogue, cast_rows, wt):
    x_refs = refs[:nx]
    w_refs = refs[nx:nx + nw]
    e_refs = refs[nx + nw:nx + nw + ne]
    o_ref = refs[nx + nw + ne]
    s_refs = refs[nx + nw + ne + 1:]
    if cast:
        @pl.when(pl.program_id(1) == 0)
        def _():
            for w_ref, s_ref in zip(w_refs, s_refs):
                def body(i, c, w_ref=w_ref, s_ref=s_ref):
                    r = pl.multiple_of(i * cast_rows, cast_rows)
                    s_ref[pl.ds(r, cast_rows), :] = w_ref[pl.ds(r, cast_rows), :].astype(BF16)
                    return c

                lax.fori_loop(0, s_ref.shape[0] // cast_rows, body, 0)
        w_src = s_refs
    else:
        w_src = w_refs
    dims = (((1,), (1 if wt else 0,)), ((), ()))
    dots = [lax.dot_general(x_refs[xi][...], w_src[wi][...], dims, preferred_element_type=F32) for xi, wi in pairs]
    o_ref[...] = epilogue(dots, [e[...] for e in e_refs]).astype(o_ref.dtype)


def _ws_matmul(xs, ws, extras, epilogue, *, tm, tn, n_out, out_dtype, pairs=None, wt=False, first_row=None, name):
    m = xs[0][0].shape[0]
    pairs = pairs or [(i, i) for i in range(len(ws))]
    first_w = ws[0][0][0] if isinstance(ws[0][0], tuple) else ws[0][0]
    cast = first_w.dtype != BF16
    wt = wt or first_row is not None
    in_specs = []
    args = []
    for a, kb, ki in xs:
        in_specs.append(pl.BlockSpec((tm, kb), lambda n, i, ki=ki: (i, ki)))
        args.append(a)
    for a, kb, ki, off in ws:
        if first_row is not None:
            a, layer = a
            assert len(ws) == 1 and kb == a.shape[2] and first_row % SUBLANES == 0 and tn % SUBLANES == 0
            assert first_row + pl.cdiv(n_out, tn) * tn <= a.shape[1]
            in_specs.append(pl.BlockSpec(
                (None, pl.Element(tn), pl.Element(kb)),
                lambda n, i, layer=layer: (layer, (first_row // SUBLANES + n * (tn // SUBLANES)) * SUBLANES, 0)))
        elif wt:
            assert not isinstance(a, tuple)
            in_specs.append(pl.BlockSpec((tn, kb), lambda n, i, ki=ki, off=off: (n + off, ki)))
        elif isinstance(a, tuple):
            a, layer = a
            in_specs.append(pl.BlockSpec((None, kb, tn), lambda n, i, ki=ki, off=off, layer=layer: (layer, ki, n + off)))
        else:
            in_specs.append(pl.BlockSpec((kb, tn), lambda n, i, ki=ki, off=off: (ki, n + off)))
        args.append(a)
    for a, kind, off in extras:
        if kind == "tile":
            in_specs.append(pl.BlockSpec((tm, tn), lambda n, i, off=off: (i, n + off)))
        else:
            in_specs.append(pl.BlockSpec((1, tn), lambda n, i, off=off: (0, n + off)))
        args.append(a)
    scratch = [pltpu.VMEM((tn, kb) if wt else (kb, tn), BF16) for _, kb, _, _ in ws] if cast else []
    cast_rows = LANES
    for sc in scratch:
        assert sc.shape[0] % cast_rows == 0, sc.shape
    kern = functools.partial(_ws_kernel, nx=len(xs), nw=len(ws), ne=len(extras), pairs=pairs, cast=cast,
                             epilogue=epilogue, cast_rows=cast_rows, wt=wt)
    return pl.pallas_call(
        kern,
        grid=(pl.cdiv(n_out, tn), m // tm),
        in_specs=in_specs,
        out_specs=pl.BlockSpec((tm, tn), lambda n, i: (i, n)),
        out_shape=jax.ShapeDtypeStruct((m, n_out), out_dtype),
        scratch_shapes=scratch,
        compiler_params=_params("parallel", "arbitrary"),
        name=name,
    )(*args)


def _ep_plain(dots, extras):
    return dots[0]


def _ep_bias_sigmoid(dots, extras):
    return jax.nn.sigmoid(dots[0] + extras[0])


def _ep_residual(dots, extras):
    return extras[0] + dots[0]


def _ep_swiglu(dots, extras):
    g, u = dots
    return g * jax.nn.sigmoid(g) * u


def _ep_merge(dots, extras):
    return (extras[0].astype(F32) * dots[0] + extras[1].astype(F32) * dots[1]
            + extras[2].astype(F32) * dots[2])


def _log_sigmoid(x):
    return jnp.minimum(x, 0.0) - jnp.log1p(jnp.exp(-jnp.abs(x)))


def _cumsum_lanes(x, seg):
    n = x.shape[-1]
    pos = lax.broadcasted_iota(jnp.int32, x.shape, 1) % seg
    shift = 1
    while shift < seg:
        x = x + jnp.where(pos >= shift, pltpu.roll(x, shift, axis=1), 0.0)
        shift *= 2
    del n
    return x


def _gates_kernel(g_ref, bias_ref, rows_ref, *, chunk):
    g = g_ref[...] + bias_ref[...]
    gt = g.T
    s = gt.shape[1]
    rows_ref[0:8, :] = _cumsum_lanes(_log_sigmoid(gt[64:72, :]), s)
    rows_ref[8:16, :] = gt[72:80, :]
    rows_ref[16:24, :] = _cumsum_lanes(_log_sigmoid(gt[80:88, :]), chunk)


def _gate_rows(g_pre, bias_vec, b, s):
    return pl.pallas_call(
        functools.partial(_gates_kernel, chunk=ML_CHUNK),
        grid=(b,),
        in_specs=[pl.BlockSpec((s, LANES), lambda i: (i, 0)),
                  pl.BlockSpec((1, LANES), lambda i: (0, 0))],
        out_specs=pl.BlockSpec((None, 24, s), lambda i: (i, 0, 0)),
        out_shape=jax.ShapeDtypeStruct((b, 24, s), F32),
        compiler_params=_params("parallel"),
        name="gate_rows",
    )(g_pre, bias_vec)


def _fox_prep_kernel(x_ref, qg_ref, kg_ref, o_ref):
    scale = FOX_DIM ** -0.5
    for h in range(2 * FOX_HEADS):
        x = x_ref[:, h * FOX_DIM:(h + 1) * FOX_DIM].astype(F32)
        y = x * lax.rsqrt(jnp.mean(x * x, axis=-1, keepdims=True) + NORM_EPS)
        if h < FOX_HEADS:
            y = y * qg_ref[...] * scale
        else:
            y = y * kg_ref[...]
        o_ref[:, h * FOX_DIM:(h + 1) * FOX_DIM] = y.astype(o_ref.dtype)


def _fox_prep(proj12, q_gain, k_gain, tm=512):
    m = proj12.shape[0]
    w = 2 * FOX_HEADS * FOX_DIM
    return pl.pallas_call(
        _fox_prep_kernel,
        grid=(m // tm,),
        in_specs=[pl.BlockSpec((tm, w), lambda i: (i, 0)),
                  pl.BlockSpec((1, FOX_DIM), lambda i: (0, 0)),
                  pl.BlockSpec((1, FOX_DIM), lambda i: (0, 0))],
        out_specs=pl.BlockSpec((tm, w), lambda i: (i, 0)),
        out_shape=jax.ShapeDtypeStruct((m, w), BF16),
        compiler_params=_params("parallel"),
        name="fox_prep",
    )(proj12, q_gain.reshape(1, -1), k_gain.reshape(1, -1))


def _flash_kernel(*refs, blk, has_bias):
    if has_bias:
        q_ref, k_ref, v_ref, d_ref, o_ref = refs
    else:
        q_ref, k_ref, v_ref, o_ref = refs
        d_ref = None
    h = pl.program_id(1)
    s_len = k_ref.shape[0]
    row = lax.broadcasted_iota(jnp.int32, (blk, blk), 0)
    col = lax.broadcasted_iota(jnp.int32, (blk, blk), 1)
    nt = (((1,), (1,)), ((), ()))
    for qi in range(s_len // blk):
        lo = qi * blk
        q = q_ref[lo:lo + blk, :]
        s_diag = lax.dot_general(q, k_ref[lo:lo + blk, :], nt, preferred_element_type=F32)
        if has_bias:
            s_diag = s_diag - d_ref[pl.ds(h, 1), lo:lo + blk]
        s_diag = jnp.where(col <= row, s_diag, -jnp.inf)
        m = jnp.max(s_diag, axis=-1, keepdims=True)
        if qi:
            s_off = lax.dot_general(q, k_ref[0:lo, :], nt, preferred_element_type=F32)
            if has_bias:
                s_off = s_off - d_ref[pl.ds(h, 1), 0:lo]
            m = jnp.maximum(m, jnp.max(s_off, axis=-1, keepdims=True))
        p_diag = jnp.exp(s_diag - m)
        l = jnp.sum(p_diag, axis=-1, keepdims=True)
        acc = jnp.dot(p_diag.astype(BF16), v_ref[lo:lo + blk, :], preferred_element_type=F32)
        if qi:
            p_off = jnp.exp(s_off - m)
            l = l + jnp.sum(p_off, axis=-1, keepdims=True)
            acc = acc + jnp.dot(p_off.astype(BF16), v_ref[0:lo, :], preferred_element_type=F32)
        o_ref[lo:lo + blk, :] = (acc / l).astype(o_ref.dtype)


def _flash(q_arr, q_off, k_arr, k_off, v_arr, v_off, dk, rows, n_heads, out_width):
    b, s, _ = q_arr.shape
    blk = ATT_BLOCK
    in_specs = [
        pl.BlockSpec((None, s, dk), lambda bi, h: (bi, 0, q_off + h)),
        pl.BlockSpec((None, s, dk), lambda bi, h: (bi, 0, k_off + h)),
        pl.BlockSpec((None, s, LANES), lambda bi, h: (bi, 0, v_off + h)),
    ]
    args = [q_arr, k_arr, v_arr]
    if rows is not None:
        in_specs.append(pl.BlockSpec((None, 8, s), lambda bi, h: (bi, 0, 0)))
        args.append(rows)
    return pl.pallas_call(
        functools.partial(_flash_kernel, blk=blk, has_bias=rows is not None),
        grid=(b, n_heads),
        in_specs=in_specs,
        out_specs=pl.BlockSpec((None, s, LANES), lambda bi, h: (bi, 0, h)),
        out_shape=jax.ShapeDtypeStruct((b, s, out_width), BF16),
        compiler_params=_params("parallel", "parallel"),
        name="flash_bias" if rows is not None else "flash",
    )(*args)


def _conv_kernel(x_ref, halo_ref, w_ref, b_ref, o_ref, xs_ref):
    tm = x_ref.shape[0]
    first = pl.program_id(1) == 0
    halo = halo_ref[...].astype(F32)
    xs_ref[0:HALO, :] = jnp.where(first, 0.0, halo)
    xs_ref[HALO:, :] = x_ref[...].astype(F32)
    y = b_ref[...]
    for j in range(ML_CONV):
        off = HALO - (ML_CONV - 1) + j
        y = y + w_ref[j:j + 1, :] * xs_ref[off:off + tm, :]
    o_ref[...] = (y * jax.nn.sigmoid(y)).astype(o_ref.dtype)


def _mlstm_conv(proj12_3d, col_block, conv_w, conv_b, tm=256):
    b, s, _ = proj12_3d.shape
    c = conv_w.shape[1]
    hb = tm // HALO
    return pl.pallas_call(
        _conv_kernel,
        grid=(b, s // tm),
        in_specs=[pl.BlockSpec((None, tm, c), lambda bi, i: (bi, i, col_block)),
                  pl.BlockSpec((None, HALO, c), lambda bi, i: (bi, jnp.maximum(i * hb - 1, 0), col_block)),
                  pl.BlockSpec((ML_CONV, c), lambda bi, i: (0, 0)),
                  pl.BlockSpec((1, c), lambda bi, i: (0, 0))],
        out_specs=pl.BlockSpec((None, tm, c), lambda bi, i: (bi, i, 0)),
        out_shape=jax.ShapeDtypeStruct((b, s, c), BF16),
        scratch_shapes=[pltpu.VMEM((tm + HALO, c), F32)],
        compiler_params=_params("parallel", "parallel"),
        name="mlstm_conv",
    )(proj12_3d, proj12_3d, conv_w, conv_b.reshape(1, c))


def _mlstm_kernel(q_ref, k_ref, v_ref, og_ref, rows_ref, gain_ref, o_ref, *, chunk):
    L = chunk
    hp = pl.program_id(1)
    s = q_ref.shape[0]
    nc = s // L

    lane = lax.broadcasted_iota(jnp.int32, (1, LANES), 1)
    row = lax.broadcasted_iota(jnp.int32, (L, L), 0)
    col = lax.broadcasted_iota(jnp.int32, (L, L), 1)
    causal = col <= row
    eye = col == row
    col1 = lax.broadcasted_iota(jnp.int32, (1, L), 1)
    row1 = lax.broadcasted_iota(jnp.int32, (L, 1), 0)

    def to_col(r):
        return jnp.sum(jnp.where(eye, r, 0.0), axis=-1, keepdims=True)

    def head_step(t0, hh, qp, kp, kp_b, state):
        c_state, n_state, m_prev = state
        head = 2 * hp + hh
        li = rows_ref[pl.ds(8 + head, 1), pl.ds(t0, L)]
        bc = rows_ref[pl.ds(16 + head, 1), pl.ds(t0, L)]
        a_row = li - bc
        bc_col = to_col(bc)
        log_w = jnp.where(causal, bc_col + a_row, -jnp.inf)
        log_inter = bc_col + m_prev
        m_t = jnp.maximum(log_inter, jnp.max(log_w, axis=-1, keepdims=True))
        w = jnp.exp(log_w - m_t)
        inter = jnp.exp(log_inter - m_t)
        head_lanes = (lane >= hh * ML_QK) & (lane < (hh + 1) * ML_QK)
        qh = jnp.where(head_lanes, qp, jnp.zeros_like(qp))
        qk = lax.dot_general(qh, kp_b, (((1,), (1,)), ((), ())), preferred_element_type=F32) * w
        vh = v_ref[pl.ds(t0, L), hh * ML_V:(hh + 1) * ML_V]
        num = (jnp.dot(qk.astype(BF16), vh, preferred_element_type=F32)
               + inter * jnp.dot(qh, c_state.astype(BF16), preferred_element_type=F32))
        den = (jnp.sum(qk, axis=-1, keepdims=True)
               + inter * jnp.sum(qh.astype(F32) * n_state, axis=-1, keepdims=True))
        h_out = num / jnp.maximum(jnp.abs(den), jnp.exp(-m_t))
        hn = h_out * lax.rsqrt(jnp.mean(h_out * h_out, axis=-1, keepdims=True) + NORM_EPS)
        hn = hn * gain_ref[:, hh * ML_V:(hh + 1) * ML_V]
        og = og_ref[pl.ds(t0, L), hh * ML_V:(hh + 1) * ML_V].astype(F32)
        o_ref[pl.ds(t0, L), hh * ML_V:(hh + 1) * ML_V] = (hn * jax.nn.sigmoid(og)).astype(o_ref.dtype)
        m_new = jnp.sum(jnp.where(row1 == L - 1, m_t, 0.0), axis=0, keepdims=True)
        b_last = jnp.sum(jnp.where(col1 == L - 1, bc, 0.0), axis=-1, keepdims=True)
        decay = jnp.exp(b_last + m_prev - m_new)
        w_last = to_col(jnp.exp(b_last + a_row - m_new))
        kw = kp * w_last
        c_upd = lax.dot_general(kw.astype(BF16), vh, (((0,), (0,)), ((), ())), preferred_element_type=F32)
        return decay * c_state + c_upd, decay * n_state + jnp.sum(kw, axis=0, keepdims=True), m_new

    states = [(jnp.zeros((LANES, ML_V), F32), jnp.zeros((1, LANES), F32), jnp.zeros((1, 1), F32))
              for _ in range(2)]
    for ci in range(nc):
        t0 = ci * L
        qp = q_ref[pl.ds(t0, L), :]
        kp = k_ref[pl.ds(t0, L), :].astype(F32) * (ML_QK ** -0.5)
        kp_b = kp.astype(BF16)
        states = [head_step(t0, hh, qp, kp, kp_b, states[hh]) for hh in range(2)]


def _mlstm(qk_conv, v_arr, v_blk0, og_arr, og_blk0, rows, out_gain):
    b, s, _ = qk_conv.shape
    npair = ML_HEADS // 2
    return pl.pallas_call(
        functools.partial(_mlstm_kernel, chunk=ML_CHUNK),
        grid=(b, npair),
        in_specs=[pl.BlockSpec((None, s, LANES), lambda bi, hp: (bi, 0, hp)),
                  pl.BlockSpec((None, s, LANES), lambda bi, hp: (bi, 0, npair + hp)),
                  pl.BlockSpec((None, s, 2 * ML_V), lambda bi, hp: (bi, 0, v_blk0 + hp)),
                  pl.BlockSpec((None, s, 2 * ML_V), lambda bi, hp: (bi, 0, og_blk0 + hp)),
                  pl.BlockSpec((None, 24, s), lambda bi, hp: (bi, 0, 0)),
                  pl.BlockSpec((1, 2 * ML_V), lambda bi, hp: (0, hp))],
        out_specs=pl.BlockSpec((None, s, 2 * ML_V), lambda bi, hp: (bi, 0, hp)),
        out_shape=jax.ShapeDtypeStruct((b, s, ML_HEADS * ML_V), BF16),
        compiler_params=_params("parallel", "parallel"),
        name="mlstm",
    )(qk_conv, qk_conv, v_arr, og_arr, rows, out_gain.reshape(1, -1))


def _mla_prep_kernel(s_ref, cos_ref, sin_ref, wuq_ref, wukv_ref, cqg_ref, ckvg_ref, qgn_ref, qgr_ref,
                     kgn_ref, kgr_ref, q_ref, k_ref, v_ref):
    scale = MLA_QK ** -0.5
    lane = lax.broadcasted_iota(jnp.int32, (1, LANES), 1)
    cosf = cos_ref[...]
    sinf = sin_ref[...]

    def norm(x, g):
        return x * lax.rsqrt(jnp.mean(x * x, axis=-1, keepdims=True) + NORM_EPS) * g

    def rope(x):
        swap = jnp.where(lane < MLA_ROPE // 2, pltpu.roll(x, LANES - MLA_ROPE // 2, axis=1),
                         pltpu.roll(x, MLA_ROPE // 2, axis=1))
        return x * cosf + swap * sinf

    cq = s_ref[:, 0:MLA_Q_RANK]
    ckv = s_ref[:, MLA_Q_RANK:MLA_Q_RANK + MLA_KV_RANK]
    kr = jnp.where(lane < MLA_ROPE, s_ref[:, MLA_Q_RANK + MLA_KV_RANK:], 0.0)
    qf = jnp.dot(norm(cq, cqg_ref[...]).astype(BF16), wuq_ref[...], preferred_element_type=F32)
    kvf = jnp.dot(norm(ckv, ckvg_ref[...]).astype(BF16), wukv_ref[...], preferred_element_type=F32)
    ss_kr = jnp.sum(kr * kr, axis=-1, keepdims=True)
    kr_g = kr * kgr_ref[...]
    nw = MLA_HEADS * MLA_NOPE
    for h in range(MLA_HEADS):
        qn = qf[:, h * MLA_NOPE:(h + 1) * MLA_NOPE]
        qr = qf[:, nw + h * LANES:nw + (h + 1) * LANES]
        ss = jnp.sum(qn * qn, axis=-1, keepdims=True) + jnp.sum(qr * qr, axis=-1, keepdims=True)
        r = lax.rsqrt(ss / MLA_QK + NORM_EPS)
        q_ref[:, h * MLA_PAD:h * MLA_PAD + MLA_NOPE] = (qn * r * qgn_ref[...] * scale).astype(q_ref.dtype)
        q_ref[:, h * MLA_PAD + MLA_NOPE:(h + 1) * MLA_PAD] = (rope(qr * r * qgr_ref[...]) * scale).astype(q_ref.dtype)
        kn = kvf[:, h * 2 * MLA_NOPE:h * 2 * MLA_NOPE + MLA_NOPE]
        rk = lax.rsqrt((jnp.sum(kn * kn, axis=-1, keepdims=True) + ss_kr) / MLA_QK + NORM_EPS)
        k_ref[:, h * MLA_PAD:h * MLA_PAD + MLA_NOPE] = (kn * rk * kgn_ref[...]).astype(k_ref.dtype)
        k_ref[:, h * MLA_PAD + MLA_NOPE:(h + 1) * MLA_PAD] = rope(kr_g * rk).astype(k_ref.dtype)
        v_ref[:, h * MLA_V:(h + 1) * MLA_V] = kvf[:, h * 2 * MLA_NOPE + MLA_NOPE:(h + 1) * 2 * MLA_NOPE].astype(v_ref.dtype)


def _mla_prep(seg3, cosf, sinf, wuq_p, wukv_b, cq_gain, ckv_gain, q_gain, k_gain, tm=256):
    m, w3 = seg3.shape
    pad = jnp.zeros((LANES - MLA_ROPE,), F32)
    qgn = q_gain[:MLA_NOPE].reshape(1, -1)
    qgr = jnp.concatenate([q_gain[MLA_NOPE:], pad]).reshape(1, -1)
    kgn = k_gain[:MLA_NOPE].reshape(1, -1)
    kgr = jnp.concatenate([k_gain[MLA_NOPE:], pad]).reshape(1, -1)
    const = lambda i: (0, 0)
    return pl.pallas_call(
        _mla_prep_kernel,
        grid=(m // tm,),
        in_specs=[pl.BlockSpec((tm, w3), lambda i: (i, 0)),
                  pl.BlockSpec((tm, LANES), lambda i: (i, 0)),
                  pl.BlockSpec((tm, LANES), lambda i: (i, 0)),
                  pl.BlockSpec(wuq_p.shape, const),
                  pl.BlockSpec(wukv_b.shape, const),
                  pl.BlockSpec((1, MLA_Q_RANK), const),
                  pl.BlockSpec((1, MLA_KV_RANK), const),
                  pl.BlockSpec((1, LANES), const),
                  pl.BlockSpec((1, LANES), const),
                  pl.BlockSpec((1, LANES), const),
                  pl.BlockSpec((1, LANES), const)],
        out_specs=[pl.BlockSpec((tm, MLA_HEADS * MLA_PAD), lambda i: (i, 0)),
                   pl.BlockSpec((tm, MLA_HEADS * MLA_PAD), lambda i: (i, 0)),
                   pl.BlockSpec((tm, MLA_HEADS * MLA_V), lambda i: (i, 0))],
        out_shape=[jax.ShapeDtypeStruct((m, MLA_HEADS * MLA_PAD), BF16),
                   jax.ShapeDtypeStruct((m, MLA_HEADS * MLA_PAD), BF16),
                   jax.ShapeDtypeStruct((m, MLA_HEADS * MLA_V), BF16)],
        compiler_params=_params("parallel"),
        name="mla_prep",
    )(seg3, cosf, sinf, wuq_p, wukv_b, cq_gain.reshape(1, -1), ckv_gain.reshape(1, -1), qgn, qgr, kgn, kgr)


_FW = FOX_HEADS * FOX_DIM
_IN_SIZES = (_FW, _FW, _FW, FOX_HEADS, ML_HEADS * ML_QK, ML_HEADS * ML_QK, ML_HEADS * ML_V,
             ML_HEADS, ML_HEADS, ML_HEADS * ML_V, MLA_Q_RANK, MLA_KV_RANK, MLA_ROPE)
_IN_OFFS = [sum(_IN_SIZES[:i]) for i in range(len(_IN_SIZES) + 1)]
(_C_FQ, _, _, _C_FF, _C_MQ, _, _, _C_MI, _C_MF, _C_MO, _C_CQ, _, _, _C_GATES) = _IN_OFFS


def _small_gate_weight(w_in_t, l):
    d = w_in_t.shape[2]
    rows = [jnp.zeros((MLA_ROPE, d), w_in_t.dtype),
            w_in_t[l, _C_FF:_C_FF + FOX_HEADS],
            w_in_t[l, _C_MI:_C_MI + ML_HEADS],
            w_in_t[l, _C_MF:_C_MF + ML_HEADS],
            jnp.zeros((LANES - MLA_ROPE - FOX_HEADS - 2 * ML_HEADS, d), w_in_t.dtype)]
    return jnp.concatenate(rows, axis=0)


def _layer(x, cosf, sinf, b, s, p, big, l):
    m, d = x.shape
    h1 = _rmsnorm(x, p["mix_norm"])
    w_in = (big["w_in_t"], l)

    def in_proj(first_col, width, out_dtype, name, extras=(), epilogue=_ep_plain):
        return _ws_matmul([(h1, d, 0)], [(w_in, d, 0, 0)], list(extras), epilogue, tm=1024, tn=512, n_out=width,
                          out_dtype=out_dtype, first_row=first_col, name=name)

    proj_fox = in_proj(_C_FQ, 3 * _FW, BF16, "proj_fox")
    proj_ml = in_proj(_C_MQ, 2 * ML_HEADS * ML_QK + ML_HEADS * ML_V, BF16, "proj_ml")
    proj_mo = in_proj(_C_MO, ML_HEADS * ML_V, BF16, "proj_mo")
    seg3 = in_proj(_C_CQ, MLA_Q_RANK + MLA_KV_RANK + LANES, F32, "proj_mla")
    gates = in_proj(_C_GATES, 3 * d, BF16, "gates", [(p["gate_bias"].reshape(1, -1), "row", 0)], _ep_bias_sigmoid)
    g_pre = _ws_matmul([(h1, d, 0)], [(_small_gate_weight(big["w_in_t"], l), d, 0, 0)], [], _ep_plain, tm=1024,
                       tn=LANES, n_out=LANES, out_dtype=F32, wt=True, name="proj_gate")

    bias_vec = jnp.concatenate([jnp.zeros((MLA_ROPE,), F32), p["fox_f_bias"], p["mlstm_i_bias"], p["mlstm_f_bias"],
                                jnp.zeros((LANES - MLA_ROPE - 24,), F32)]).reshape(1, LANES)
    rows = _gate_rows(g_pre, bias_vec, b, s)

    qk_fox = _fox_prep(proj_fox, p["fox_q_gain"], p["fox_k_gain"]).reshape(b, s, -1)
    o_fox = _flash(qk_fox, 0, qk_fox, FOX_HEADS, proj_fox.reshape(b, s, -1), 2 * FOX_HEADS, FOX_DIM, rows, FOX_HEADS,
                   FOX_HEADS * FOX_DIM).reshape(m, -1)
    proj_ml_3d = proj_ml.reshape(b, s, -1)
    qk_conv = _mlstm_conv(proj_ml_3d, 0, p["mlstm_conv_w"], p["mlstm_conv_b"])
    o_ml = _mlstm(qk_conv, proj_ml_3d, 2 * ML_HEADS * ML_QK // (2 * ML_V), proj_mo.reshape(b, s, -1), 0, rows,
                  p["mlstm_out_gain"]).reshape(m, -1)
    wuq = p["mla_w_uq"].reshape(MLA_Q_RANK, MLA_HEADS, MLA_QK)
    wuq_p = jnp.concatenate([
        wuq[:, :, :MLA_NOPE].reshape(MLA_Q_RANK, -1),
        jnp.pad(wuq[:, :, MLA_NOPE:], ((0, 0), (0, 0), (0, LANES - MLA_ROPE))).reshape(MLA_Q_RANK, -1)],
        axis=1).astype(BF16)
    q_mla, k_mla, v_mla = _mla_prep(seg3, cosf, sinf, wuq_p, p["mla_w_ukv"].astype(BF16), p["mla_cq_gain"],
                                    p["mla_ckv_gain"], p["mla_q_gain"], p["mla_k_gain"])
    o_mla = _flash(q_mla.reshape(b, s, -1), 0, k_mla.reshape(b, s, -1), 0, v_mla.reshape(b, s, -1), 0,
                   MLA_PAD, None, MLA_HEADS, MLA_HEADS * MLA_V).reshape(m, -1)

    kb = o_fox.shape[1]
    nb = d // 512
    merged = _ws_matmul([(o_fox, kb, 0), (o_ml, kb, 0), (o_mla, kb, 0)],
                        [((big["w_fox_out"], l), kb, 0, 0), ((big["w_mlstm_out"], l), kb, 0, 0),
                         ((big["w_mla_out"], l), kb, 0, 0)],
                        [(gates, "tile", 0), (gates, "tile", nb), (gates, "tile", 2 * nb)],
                        _ep_merge, tm=1024, tn=512, n_out=d, out_dtype=BF16, name="merge")
    x = _ws_matmul([(merged, d, 0)], [((big["w_o"], l), d, 0, 0)], [(x, "tile", 0)], _ep_residual,
                   tm=1024, tn=512, n_out=d, out_dtype=F32, name="w_o")

    h2 = _rmsnorm(x, p["ffn_norm"])
    d_ff = big["w_gate"].shape[2]
    act = _ws_matmul([(h2, d, 0)], [((big["w_gate"], l), d, 0, 0), ((big["w_up"], l), d, 0, 0)], [], _ep_swiglu,
                     tm=1024, tn=256, n_out=d_ff, out_dtype=BF16, pairs=[(0, 0), (0, 1)], name="gate_up")
    kh = d_ff // 2
    for ki in range(2):
        x = _ws_matmul([(act, kh, ki)], [((big["w_down"], l), kh, ki, 0)], [(x, "tile", 0)], _ep_residual,
                       tm=512, tn=512, n_out=d, out_dtype=F32, name=f"down{ki}")
    return x


def kernel(x, positions, mix_norm, w_in, fox_f_bias, fox_q_gain, fox_k_gain, mlstm_conv_w, mlstm_conv_b,
           mlstm_i_bias, mlstm_f_bias, mlstm_out_gain, mla_cq_gain, mla_ckv_gain, mla_w_uq, mla_w_ukv,
           mla_q_gain, mla_k_gain, w_fox_out, w_mlstm_out, w_mla_out, gate_bias, w_o, ffn_norm, w_gate,
           w_up, w_down):
    b, s, d = x.shape
    depth = w_in.shape[0]
    inv_freq = jnp.power(ROPE_THETA, -jnp.arange(0, MLA_ROPE, 2, dtype=F32) / MLA_ROPE)
    ang = positions.astype(F32).reshape(b * s, 1) * inv_freq
    zpad = jnp.zeros((b * s, LANES - MLA_ROPE), F32)
    cosf = jnp.concatenate([jnp.cos(ang), jnp.cos(ang), zpad], axis=1)
    sinf = jnp.concatenate([-jnp.sin(ang), jnp.sin(ang), zpad], axis=1)
    names = dict(mix_norm=mix_norm, fox_f_bias=fox_f_bias, fox_q_gain=fox_q_gain, fox_k_gain=fox_k_gain,
                 mlstm_conv_w=mlstm_conv_w, mlstm_conv_b=mlstm_conv_b, mlstm_i_bias=mlstm_i_bias,
                 mlstm_f_bias=mlstm_f_bias, mlstm_out_gain=mlstm_out_gain, mla_cq_gain=mla_cq_gain,
                 mla_ckv_gain=mla_ckv_gain, mla_w_uq=mla_w_uq, mla_w_ukv=mla_w_ukv, mla_q_gain=mla_q_gain,
                 mla_k_gain=mla_k_gain, gate_bias=gate_bias, ffn_norm=ffn_norm)
    big = dict(w_in_t=jnp.swapaxes(w_in, 1, 2), w_fox_out=w_fox_out, w_mlstm_out=w_mlstm_out, w_mla_out=w_mla_out, w_o=w_o, w_gate=w_gate,
               w_up=w_up, w_down=w_down)
    xf = x.reshape(b * s, d)
    for l in range(depth):
        xf = _layer(xf, cosf, sinf, b, s, {k: v[l] for k, v in names.items()}, big, l)
    return xf.reshape(b, s, d)
```

```python
import functools

import jax
import jax.numpy as jnp
from jax import lax
from jax.experimental import pallas as pl
from jax.experimental.pallas import tpu as pltpu

F32 = jnp.float32
BF16 = jnp.bfloat16

NORM_EPS = 1e-6
ROPE_THETA = 10000.0

FOX_HEADS = 8
FOX_DIM = 128
ML_HEADS = 8
ML_QK = 64
ML_V = 128
ML_CONV = 4
MLA_HEADS = 8
MLA_Q_RANK = 896
MLA_KV_RANK = 512
MLA_NOPE = 128
MLA_ROPE = 64
MLA_V = 128
MLA_QK = MLA_NOPE + MLA_ROPE
MLA_PAD = 256

LANES = 128
SUBLANES = 8
VMEM_LIMIT = 56 * 1024 * 1024
ML_CHUNK = 256
ATT_BLOCK = 256
HALO = 16


def _params(*sem):
    return pltpu.CompilerParams(dimension_semantics=sem, vmem_limit_bytes=VMEM_LIMIT)


def _rmsnorm_kernel(x_ref, g_ref, o_ref):
    x = x_ref[...]
    ms = jnp.mean(x * x, axis=-1, keepdims=True)
    o_ref[...] = (x * lax.rsqrt(ms + NORM_EPS) * g_ref[...]).astype(o_ref.dtype)


def _rmsnorm(x, gain, tm=256):
    m, d = x.shape
    return pl.pallas_call(
        _rmsnorm_kernel,
        grid=(m // tm,),
        in_specs=[pl.BlockSpec((tm, d), lambda i: (i, 0)),
                  pl.BlockSpec((1, d), lambda i: (0, 0))],
        out_specs=pl.BlockSpec((tm, d), lambda i: (i, 0)),
        out_shape=jax.ShapeDtypeStruct((m, d), BF16),
        compiler_params=_params("parallel"),
        name="rmsnorm",
    )(x, gain.reshape(1, d))


def _ws_kernel(*refs, nx, nw, ne, pairs, epilogue, cast_rows, wt, panel_src):
    x_refs = refs[:nx]
    w_hbm = refs[nx:nx + nw]
    e_refs = refs[nx + nw:nx + nw + ne]
    o_ref = refs[nx + nw + ne]
    scratch = refs[nx + nw + ne + 1:]
    stage, w_bf, sem = scratch[:nw], scratch[nw:2 * nw], scratch[2 * nw]
    n = pl.program_id(0)
    n_panels = pl.num_programs(0)

    def panel_copy(wi, panel):
        return pltpu.make_async_copy(panel_src[wi](w_hbm[wi], panel), stage[wi], sem.at[wi])

    @pl.when(pl.program_id(1) == 0)
    def _():
        @pl.when(n == 0)
        def _():
            for wi in range(nw):
                panel_copy(wi, 0).start()

        for wi in range(nw):
            panel_copy(wi, n).wait()

            def body(i, c, wi=wi):
                r = pl.multiple_of(i * cast_rows, cast_rows)
                w_bf[wi][pl.ds(r, cast_rows), :] = stage[wi][pl.ds(r, cast_rows), :].astype(BF16)
                return c

            lax.fori_loop(0, stage[wi].shape[0] // cast_rows, body, 0)

        @pl.when(n + 1 < n_panels)
        def _():
            for wi in range(nw):
                panel_copy(wi, n + 1).start()

    dims = (((1,), (1 if wt else 0,)), ((), ()))
    dots = [lax.dot_general(x_refs[xi][...], w_bf[wi][...], dims, preferred_element_type=F32) for xi, wi in pairs]
    o_ref[...] = epilogue(dots, [e[...] for e in e_refs]).astype(o_ref.dtype)


def _ws_matmul(xs, ws, extras, epilogue, *, tm, tn, n_out, out_dtype, pairs=None, wt=False, first_row=None, name):
    m = xs[0][0].shape[0]
    assert n_out % tn == 0 and m % tm == 0
    pairs = pairs or [(i, i) for i in range(len(ws))]
    wt = wt or first_row is not None
    in_specs = []
    args = []
    panel_src = []
    for a, kb, ki in xs:
        in_specs.append(pl.BlockSpec((tm, kb), lambda n, i, ki=ki: (i, ki)))
        args.append(a)
    for a, kb, ki, off in ws:
        layer = None
        if isinstance(a, tuple):
            a, layer = a
        assert a.dtype == F32
        if first_row is not None:
            assert len(ws) == 1 and kb == a.shape[2] and first_row % SUBLANES == 0 and tn % SUBLANES == 0
            assert first_row + n_out <= a.shape[1]
            panel_src.append(lambda w, p, layer=layer: w.at[
                layer, pl.ds(pl.multiple_of(first_row + p * tn, SUBLANES), tn), :])
        elif wt:
            assert layer is None and kb == a.shape[1]
            panel_src.append(lambda w, p: w.at[pl.ds(pl.multiple_of(p * tn, SUBLANES), tn), :])
        else:
            assert layer is not None
            panel_src.append(lambda w, p, layer=layer, kb=kb, ki=ki, off=off: w.at[
                layer, pl.ds(ki * kb, kb), pl.ds(pl.multiple_of((p + off) * tn, LANES), tn)])
        in_specs.append(pl.BlockSpec(memory_space=pl.ANY))
        args.append(a)
    for a, kind, off in extras:
        if kind == "tile":
            in_specs.append(pl.BlockSpec((tm, tn), lambda n, i, off=off: (i, n + off)))
        else:
            in_specs.append(pl.BlockSpec((1, tn), lambda n, i, off=off: (0, n + off)))
        args.append(a)
    panel_shapes = [(tn, kb) if wt else (kb, tn) for _, kb, _, _ in ws]
    cast_rows = LANES
    for shp in panel_shapes:
        assert shp[0] % cast_rows == 0, shp
    scratch = ([pltpu.VMEM(shp, F32) for shp in panel_shapes] + [pltpu.VMEM(shp, BF16) for shp in panel_shapes]
               + [pltpu.SemaphoreType.DMA((len(ws),))])
    kern = functools.partial(_ws_kernel, nx=len(xs), nw=len(ws), ne=len(extras), pairs=pairs, epilogue=epilogue,
                             cast_rows=cast_rows, wt=wt, panel_src=panel_src)
    return pl.pallas_call(
        kern,
        grid=(n_out // tn, m // tm),
        in_specs=in_specs,
        out_specs=pl.BlockSpec((tm, tn), lambda n, i: (i, n)),
        out_shape=jax.ShapeDtypeStruct((m, n_out), out_dtype),
        scratch_shapes=scratch,
        compiler_params=_params("arbitrary", "arbitrary"),
        name=name,
    )(*args)


def _ep_plain(dots, extras):
    return dots[0]


def _ep_bias_sigmoid(dots, extras):
    return jax.nn.sigmoid(dots[0] + extras[0])


def _ep_residual(dots, extras):
    return extras[0] + dots[0]


def _ep_swiglu(dots, extras):
    g, u = dots
    return g * jax.nn.sigmoid(g) * u


def _ep_merge(dots, extras):
    return (extras[0].astype(F32) * dots[0] + extras[1].astype(F32) * dots[1]
            + extras[2].astype(F32) * dots[2])


def _log_sigmoid(x):
    return jnp.minimum(x, 0.0) - jnp.log1p(jnp.exp(-jnp.abs(x)))


def _cumsum_lanes(x, seg):
    n = x.shape[-1]
    pos = lax.broadcasted_iota(jnp.int32, x.shape, 1) % seg
    shift = 1
    while shift < seg:
        x = x + jnp.where(pos >= shift, pltpu.roll(x, shift, axis=1), 0.0)
        shift *= 2
    del n
    return x


def _gates_kernel(g_ref, bias_ref, rows_ref, *, chunk):
    g = g_ref[...] + bias_ref[...]
    gt = g.T
    s = gt.shape[1]
    rows_ref[0:8, :] = _cumsum_lanes(_log_sigmoid(gt[64:72, :]), s)
    rows_ref[8:16, :] = gt[72:80, :]
    rows_ref[16:24, :] = _cumsum_lanes(_log_sigmoid(gt[80:88, :]), chunk)


def _gate_rows(g_pre, bias_vec, b, s):
    return pl.pallas_call(
        functools.partial(_gates_kernel, chunk=ML_CHUNK),
        grid=(b,),
        in_specs=[pl.BlockSpec((s, LANES), lambda i: (i, 0)),
                  pl.BlockSpec((1, LANES), lambda i: (0, 0))],
        out_specs=pl.BlockSpec((None, 24, s), lambda i: (i, 0, 0)),
        out_shape=jax.ShapeDtypeStruct((b, 24, s), F32),
        compiler_params=_params("parallel"),
        name="gate_rows",
    )(g_pre, bias_vec)


def _fox_prep_kernel(x_ref, qg_ref, kg_ref, o_ref):
    scale = FOX_DIM ** -0.5
    for h in range(2 * FOX_HEADS):
        x = x_ref[:, h * FOX_DIM:(h + 1) * FOX_DIM].astype(F32)
        y = x * lax.rsqrt(jnp.mean(x * x, axis=-1, keepdims=True) + NORM_EPS)
        if h < FOX_HEADS:
            y = y * qg_ref[...] * scale
        else:
            y = y * kg_ref[...]
        o_ref[:, h * FOX_DIM:(h + 1) * FOX_DIM] = y.astype(o_ref.dtype)


def _fox_prep(proj12, q_gain, k_gain, tm=512):
    m = proj12.shape[0]
    w = 2 * FOX_HEADS * FOX_DIM
    return pl.pallas_call(
        _fox_prep_kernel,
        grid=(m // tm,),
        in_specs=[pl.BlockSpec((tm, w), lambda i: (i, 0)),
                  pl.BlockSpec((1, FOX_DIM), lambda i: (0, 0)),
                  pl.BlockSpec((1, FOX_DIM), lambda i: (0, 0))],
        out_specs=pl.BlockSpec((tm, w), lambda i: (i, 0)),
        out_shape=jax.ShapeDtypeStruct((m, w), BF16),
        compiler_params=_params("parallel"),
        name="fox_prep",
    )(proj12, q_gain.reshape(1, -1), k_gain.reshape(1, -1))


def _flash_kernel(*refs, blk, has_bias):
    if has_bias:
        q_ref, k_ref, v_ref, d_ref, o_ref = refs
    else:
        q_ref, k_ref, v_ref, o_ref = refs
        d_ref = None
    h = pl.program_id(1)
    s_len = k_ref.shape[0]
    row = lax.broadcasted_iota(jnp.int32, (blk, blk), 0)
    col = lax.broadcasted_iota(jnp.int32, (blk, blk), 1)
    nt = (((1,), (1,)), ((), ()))
    for qi in range(s_len // blk):
        lo = qi * blk
        q = q_ref[lo:lo + blk, :]
        s_diag = lax.dot_general(q, k_ref[lo:lo + blk, :], nt, preferred_element_type=F32)
        if has_bias:
            s_diag = s_diag - d_ref[pl.ds(h, 1), lo:lo + blk]
        s_diag = jnp.where(col <= row, s_diag, -jnp.inf)
        m = jnp.max(s_diag, axis=-1, keepdims=True)
        if qi:
            s_off = lax.dot_general(q, k_ref[0:lo, :], nt, preferred_element_type=F32)
            if has_bias:
                s_off = s_off - d_ref[pl.ds(h, 1), 0:lo]
            m = jnp.maximum(m, jnp.max(s_off, axis=-1, keepdims=True))
        p_diag = jnp.exp(s_diag - m)
        l = jnp.sum(p_diag, axis=-1, keepdims=True)
        acc = jnp.dot(p_diag.astype(BF16), v_ref[lo:lo + blk, :], preferred_element_type=F32)
        if qi:
            p_off = jnp.exp(s_off - m)
            l = l + jnp.sum(p_off, axis=-1, keepdims=True)
            acc = acc + jnp.dot(p_off.astype(BF16), v_ref[0:lo, :], preferred_element_type=F32)
        o_ref[lo:lo + blk, :] = (acc / l).astype(o_ref.dtype)


def _flash(q_arr, q_off, k_arr, k_off, v_arr, v_off, dk, rows, n_heads, out_width):
    b, s, _ = q_arr.shape
    blk = ATT_BLOCK
    in_specs = [
        pl.BlockSpec((None, s, dk), lambda bi, h: (bi, 0, q_off + h)),
        pl.BlockSpec((None, s, dk), lambda bi, h: (bi, 0, k_off + h)),
        pl.BlockSpec((None, s, LANES), lambda bi, h: (bi, 0, v_off + h)),
    ]
    args = [q_arr, k_arr, v_arr]
    if rows is not None:
        in_specs.append(pl.BlockSpec((None, 8, s), lambda bi, h: (bi, 0, 0)))
        args.append(rows)
    return pl.pallas_call(
        functools.partial(_flash_kernel, blk=blk, has_bias=rows is not None),
        grid=(b, n_heads),
        in_specs=in_specs,
        out_specs=pl.BlockSpec((None, s, LANES), lambda bi, h: (bi, 0, h)),
        out_shape=jax.ShapeDtypeStruct((b, s, out_width), BF16),
        compiler_params=_params("parallel", "parallel"),
        name="flash_bias" if rows is not None else "flash",
    )(*args)


def _conv_kernel(x_ref, halo_ref, w_ref, b_ref, o_ref, xs_ref):
    tm = x_ref.shape[0]
    first = pl.program_id(1) == 0
    halo = halo_ref[...].astype(F32)
    xs_ref[0:HALO, :] = jnp.where(first, 0.0, halo)
    xs_ref[HALO:, :] = x_ref[...].astype(F32)
    y = b_ref[...]
    for j in range(ML_CONV):
        off = HALO - (ML_CONV - 1) + j
        y = y + w_ref[j:j + 1, :] * xs_ref[off:off + tm, :]
    o_ref[...] = (y * jax.nn.sigmoid(y)).astype(o_ref.dtype)


def _mlstm_conv(proj12_3d, col_block, conv_w, conv_b, tm=256):
    b, s, _ = proj12_3d.shape
    c = conv_w.shape[1]
    hb = tm // HALO
    return pl.pallas_call(
        _conv_kernel,
        grid=(b, s // tm),
        in_specs=[pl.BlockSpec((None, tm, c), lambda bi, i: (bi, i, col_block)),
                  pl.BlockSpec((None, HALO, c), lambda bi, i: (bi, jnp.maximum(i * hb - 1, 0), col_block)),
                  pl.BlockSpec((ML_CONV, c), lambda bi, i: (0, 0)),
                  pl.BlockSpec((1, c), lambda bi, i: (0, 0))],
        out_specs=pl.BlockSpec((None, tm, c), lambda bi, i: (bi, i, 0)),
        out_shape=jax.ShapeDtypeStruct((b, s, c), BF16),
        scratch_shapes=[pltpu.VMEM((tm + HALO, c), F32)],
        compiler_params=_params("parallel", "parallel"),
        name="mlstm_conv",
    )(proj12_3d, proj12_3d, conv_w, conv_b.reshape(1, c))


def _mlstm_kernel(q_ref, k_ref, v_ref, og_ref, rows_ref, gain_ref, o_ref, *, chunk):
    L = chunk
    hp = pl.program_id(1)
    s = q_ref.shape[0]
    nc = s // L

    lane = lax.broadcasted_iota(jnp.int32, (1, LANES), 1)
    row = lax.broadcasted_iota(jnp.int32, (L, L), 0)
    col = lax.broadcasted_iota(jnp.int32, (L, L), 1)
    causal = col <= row
    eye = col == row
    col1 = lax.broadcasted_iota(jnp.int32, (1, L), 1)
    row1 = lax.broadcasted_iota(jnp.int32, (L, 1), 0)

    def to_col(r):
        return jnp.sum(jnp.where(eye, r, 0.0), axis=-1, keepdims=True)

    def head_step(t0, hh, qp, kp, kp_b, state):
        c_state, n_state, m_prev = state
        head = 2 * hp + hh
        li = rows_ref[pl.ds(8 + head, 1), pl.ds(t0, L)]
        bc = rows_ref[pl.ds(16 + head, 1), pl.ds(t0, L)]
        a_row = li - bc
        bc_col = to_col(bc)
        log_w = jnp.where(causal, bc_col + a_row, -jnp.inf)
        log_inter = bc_col + m_prev
        m_t = jnp.maximum(log_inter, jnp.max(log_w, axis=-1, keepdims=True))
        w = jnp.exp(log_w - m_t)
        inter = jnp.exp(log_inter - m_t)
        head_lanes = (lane >= hh * ML_QK) & (lane < (hh + 1) * ML_QK)
        qh = jnp.where(head_lanes, qp, jnp.zeros_like(qp))
        qk = lax.dot_general(qh, kp_b, (((1,), (1,)), ((), ())), preferred_element_type=F32) * w
        vh = v_ref[pl.ds(t0, L), hh * ML_V:(hh + 1) * ML_V]
        num = (jnp.dot(qk.astype(BF16), vh, preferred_element_type=F32)
               + inter * jnp.dot(qh, c_state.astype(BF16), preferred_element_type=F32))
        den = (jnp.sum(qk, axis=-1, keepdims=True)
               + inter * jnp.sum(qh.astype(F32) * n_state, axis=-1, keepdims=True))
        h_out = num / jnp.maximum(jnp.abs(den), jnp.exp(-m_t))
        hn = h_out * lax.rsqrt(jnp.mean(h_out * h_out, axis=-1, keepdims=True) + NORM_EPS)
        hn = hn * gain_ref[:, hh * ML_V:(hh + 1) * ML_V]
        og = og_ref[pl.ds(t0, L), hh * ML_V:(hh + 1) * ML_V].astype(F32)
        o_ref[pl.ds(t0, L), hh * ML_V:(hh + 1) * ML_V] = (hn * jax.nn.sigmoid(og)).astype(o_ref.dtype)
        m_new = jnp.sum(jnp.where(row1 == L - 1, m_t, 0.0), axis=0, keepdims=True)
        b_last = jnp.sum(jnp.where(col1 == L - 1, bc, 0.0), axis=-1, keepdims=True)
        decay = jnp.exp(b_last + m_prev - m_new)
        w_last = to_col(jnp.exp(b_last + a_row - m_new))
        kw = kp * w_last
        c_upd = lax.dot_general(kw.astype(BF16), vh, (((0,), (0,)), ((), ())), preferred_element_type=F32)
        return decay * c_state + c_upd, decay * n_state + jnp.sum(kw, axis=0, keepdims=True), m_new

    states = [(jnp.zeros((LANES, ML_V), F32), jnp.zeros((1, LANES), F32), jnp.zeros((1, 1), F32))
              for _ in range(2)]
    for ci in range(nc):
        t0 = ci * L
        qp = q_ref[pl.ds(t0, L), :]
        kp = k_ref[pl.ds(t0, L), :].astype(F32) * (ML_QK ** -0.5)
        kp_b = kp.astype(BF16)
        states = [head_step(t0, hh, qp, kp, kp_b, states[hh]) for hh in range(2)]


def _mlstm(qk_conv, v_arr, v_blk0, og_arr, og_blk0, rows, out_gain):
    b, s, _ = qk_conv.shape
    npair = ML_HEADS // 2
    return pl.pallas_call(
        functools.partial(_mlstm_kernel, chunk=ML_CHUNK),
        grid=(b, npair),
        in_specs=[pl.BlockSpec((None, s, LANES), lambda bi, hp: (bi, 0, hp)),
                  pl.BlockSpec((None, s, LANES), lambda bi, hp: (bi, 0, npair + hp)),
                  pl.BlockSpec((None, s, 2 * ML_V), lambda bi, hp: (bi, 0, v_blk0 + hp)),
                  pl.BlockSpec((None, s, 2 * ML_V), lambda bi, hp: (bi, 0, og_blk0 + hp)),
                  pl.BlockSpec((None, 24, s), lambda bi, hp: (bi, 0, 0)),
                  pl.BlockSpec((1, 2 * ML_V), lambda bi, hp: (0, hp))],
        out_specs=pl.BlockSpec((None, s, 2 * ML_V), lambda bi, hp: (bi, 0, hp)),
        out_shape=jax.ShapeDtypeStruct((b, s, ML_HEADS * ML_V), BF16),
        compiler_params=_params("parallel", "parallel"),
        name="mlstm",
    )(qk_conv, qk_conv, v_arr, og_arr, rows, out_gain.reshape(1, -1))


def _mla_prep_kernel(s_ref, cos_ref, sin_ref, wuq_ref, wukv_ref, cqg_ref, ckvg_ref, qgn_ref, qgr_ref,
                     kgn_ref, kgr_ref, q_ref, k_ref, v_ref):
    scale = MLA_QK ** -0.5
    lane = lax.broadcasted_iota(jnp.int32, (1, LANES), 1)
    cosf = cos_ref[...]
    sinf = sin_ref[...]

    def norm(x, g):
        return x * lax.rsqrt(jnp.mean(x * x, axis=-1, keepdims=True) + NORM_EPS) * g

    def rope(x):
        swap = jnp.where(lane < MLA_ROPE // 2, pltpu.roll(x, LANES - MLA_ROPE // 2, axis=1),
                         pltpu.roll(x, MLA_ROPE // 2, axis=1))
        return x * cosf + swap * sinf

    cq = s_ref[:, 0:MLA_Q_RANK]
    ckv = s_ref[:, MLA_Q_RANK:MLA_Q_RANK + MLA_KV_RANK]
    kr = jnp.where(lane < MLA_ROPE, s_ref[:, MLA_Q_RANK + MLA_KV_RANK:], 0.0)
    qf = jnp.dot(norm(cq, cqg_ref[...]).astype(BF16), wuq_ref[...], preferred_element_type=F32)
    kvf = jnp.dot(norm(ckv, ckvg_ref[...]).astype(BF16), wukv_ref[...], preferred_element_type=F32)
    ss_kr = jnp.sum(kr * kr, axis=-1, keepdims=True)
    kr_g = kr * kgr_ref[...]
    nw = MLA_HEADS * MLA_NOPE
    for h in range(MLA_HEADS):
        qn = qf[:, h * MLA_NOPE:(h + 1) * MLA_NOPE]
        qr = qf[:, nw + h * LANES:nw + (h + 1) * LANES]
        ss = jnp.sum(qn * qn, axis=-1, keepdims=True) + jnp.sum(qr * qr, axis=-1, keepdims=True)
        r = lax.rsqrt(ss / MLA_QK + NORM_EPS)
        q_ref[:, h * MLA_PAD:h * MLA_PAD + MLA_NOPE] = (qn * r * qgn_ref[...] * scale).astype(q_ref.dtype)
        q_ref[:, h * MLA_PAD + MLA_NOPE:(h + 1) * MLA_PAD] = (rope(qr * r * qgr_ref[...]) * scale).astype(q_ref.dtype)
        kn = kvf[:, h * 2 * MLA_NOPE:h * 2 * MLA_NOPE + MLA_NOPE]
        rk = lax.rsqrt((jnp.sum(kn * kn, axis=-1, keepdims=True) + ss_kr) / MLA_QK + NORM_EPS)
        k_ref[:, h * MLA_PAD:h * MLA_PAD + MLA_NOPE] = (kn * rk * kgn_ref[...]).astype(k_ref.dtype)
        k_ref[:, h * MLA_PAD + MLA_NOPE:(h + 1) * MLA_PAD] = rope(kr_g * rk).astype(k_ref.dtype)
        v_ref[:, h * MLA_V:(h + 1) * MLA_V] = kvf[:, h * 2 * MLA_NOPE + MLA_NOPE:(h + 1) * 2 * MLA_NOPE].astype(v_ref.dtype)


def _mla_prep(seg3, cosf, sinf, wuq_p, wukv_b, cq_gain, ckv_gain, q_gain, k_gain, tm=256):
    m, w3 = seg3.shape
    pad = jnp.zeros((LANES - MLA_ROPE,), F32)
    qgn = q_gain[:MLA_NOPE].reshape(1, -1)
    qgr = jnp.concatenate([q_gain[MLA_NOPE:], pad]).reshape(1, -1)
    kgn = k_gain[:MLA_NOPE].reshape(1, -1)
    kgr = jnp.concatenate([k_gain[MLA_NOPE:], pad]).reshape(1, -1)
    const = lambda i: (0, 0)
    return pl.pallas_call(
        _mla_prep_kernel,
        grid=(m // tm,),
        in_specs=[pl.BlockSpec((tm, w3), lambda i: (i, 0)),
                  pl.BlockSpec((tm, LANES), lambda i: (i, 0)),
                  pl.BlockSpec((tm, LANES), lambda i: (i, 0)),
                  pl.BlockSpec(wuq_p.shape, const),
                  pl.BlockSpec(wukv_b.shape, const),
                  pl.BlockSpec((1, MLA_Q_RANK), const),
                  pl.BlockSpec((1, MLA_KV_RANK), const),
                  pl.BlockSpec((1, LANES), const),
                  pl.BlockSpec((1, LANES), const),
                  pl.BlockSpec((1, LANES), const),
                  pl.BlockSpec((1, LANES), const)],
        out_specs=[pl.BlockSpec((tm, MLA_HEADS * MLA_PAD), lambda i: (i, 0)),
                   pl.BlockSpec((tm, MLA_HEADS * MLA_PAD), lambda i: (i, 0)),
                   pl.BlockSpec((tm, MLA_HEADS * MLA_V), lambda i: (i, 0))],
        out_shape=[jax.ShapeDtypeStruct((m, MLA_HEADS * MLA_PAD), BF16),
                   jax.ShapeDtypeStruct((m, MLA_HEADS * MLA_PAD), BF16),
                   jax.ShapeDtypeStruct((m, MLA_HEADS * MLA_V), BF16)],
        compiler_params=_params("parallel"),
        name="mla_prep",
    )(seg3, cosf, sinf, wuq_p, wukv_b, cq_gain.reshape(1, -1), ckv_gain.reshape(1, -1), qgn, qgr, kgn, kgr)


_FW = FOX_HEADS * FOX_DIM
_IN_SIZES = (_FW, _FW, _FW, FOX_HEADS, ML_HEADS * ML_QK, ML_HEADS * ML_QK, ML_HEADS * ML_V,
             ML_HEADS, ML_HEADS, ML_HEADS * ML_V, MLA_Q_RANK, MLA_KV_RANK, MLA_ROPE)
_IN_OFFS = [sum(_IN_SIZES[:i]) for i in range(len(_IN_SIZES) + 1)]
(_C_FQ, _, _, _C_FF, _C_MQ, _, _, _C_MI, _C_MF, _C_MO, _C_CQ, _, _, _C_GATES) = _IN_OFFS


def _small_gate_weight(w_in_t, l):
    d = w_in_t.shape[2]
    rows = [jnp.zeros((MLA_ROPE, d), w_in_t.dtype),
            w_in_t[l, _C_FF:_C_FF + FOX_HEADS],
            w_in_t[l, _C_MI:_C_MI + ML_HEADS],
            w_in_t[l, _C_MF:_C_MF + ML_HEADS],
            jnp.zeros((LANES - MLA_ROPE - FOX_HEADS - 2 * ML_HEADS, d), w_in_t.dtype)]
    return jnp.concatenate(rows, axis=0)


def _layer(x, cosf, sinf, b, s, p, big, l):
    m, d = x.shape
    h1 = _rmsnorm(x, p["mix_norm"])
    w_in = (big["w_in_t"], l)

    def in_proj(first_col, width, out_dtype, name, extras=(), epilogue=_ep_plain, tn=1024):
        return _ws_matmul([(h1, d, 0)], [(w_in, d, 0, 0)], list(extras), epilogue, tm=1024, tn=tn, n_out=width,
                          out_dtype=out_dtype, first_row=first_col, name=name)

    proj_fox = in_proj(_C_FQ, 3 * _FW, BF16, "proj_fox")
    proj_ml = in_proj(_C_MQ, 2 * ML_HEADS * ML_QK + ML_HEADS * ML_V, BF16, "proj_ml")
    proj_mo = in_proj(_C_MO, ML_HEADS * ML_V, BF16, "proj_mo")
    w_mla = MLA_Q_RANK + MLA_KV_RANK + LANES
    seg3 = in_proj(_C_CQ, w_mla, F32, "proj_mla", tn=w_mla // 2)
    gates = in_proj(_C_GATES, 3 * d, BF16, "gates", [(p["gate_bias"].reshape(1, -1), "row", 0)], _ep_bias_sigmoid)
    g_pre = _ws_matmul([(h1, d, 0)], [(_small_gate_weight(big["w_in_t"], l), d, 0, 0)], [], _ep_plain, tm=1024,
                       tn=LANES, n_out=LANES, out_dtype=F32, wt=True, name="proj_gate")

    bias_vec = jnp.concatenate([jnp.zeros((MLA_ROPE,), F32), p["fox_f_bias"], p["mlstm_i_bias"], p["mlstm_f_bias"],
                                jnp.zeros((LANES - MLA_ROPE - 24,), F32)]).reshape(1, LANES)
    rows = _gate_rows(g_pre, bias_vec, b, s)

    qk_fox = _fox_prep(proj_fox, p["fox_q_gain"], p["fox_k_gain"]).reshape(b, s, -1)
    o_fox = _flash(qk_fox, 0, qk_fox, FOX_HEADS, proj_fox.reshape(b, s, -1), 2 * FOX_HEADS, FOX_DIM, rows, FOX_HEADS,
                   FOX_HEADS * FOX_DIM).reshape(m, -1)
    proj_ml_3d = proj_ml.reshape(b, s, -1)
    qk_conv = _mlstm_conv(proj_ml_3d, 0, p["mlstm_conv_w"], p["mlstm_conv_b"])
    o_ml = _mlstm(qk_conv, proj_ml_3d, 2 * ML_HEADS * ML_QK // (2 * ML_V), proj_mo.reshape(b, s, -1), 0, rows,
                  p["mlstm_out_gain"]).reshape(m, -1)
    wuq = p["mla_w_uq"].reshape(MLA_Q_RANK, MLA_HEADS, MLA_QK)
    wuq_p = jnp.concatenate([
        wuq[:, :, :MLA_NOPE].reshape(MLA_Q_RANK, -1),
        jnp.pad(wuq[:, :, MLA_NOPE:], ((0, 0), (0, 0), (0, LANES - MLA_ROPE))).reshape(MLA_Q_RANK, -1)],
        axis=1).astype(BF16)
    q_mla, k_mla, v_mla = _mla_prep(seg3, cosf, sinf, wuq_p, p["mla_w_ukv"].astype(BF16), p["mla_cq_gain"],
                                    p["mla_ckv_gain"], p["mla_q_gain"], p["mla_k_gain"])
    o_mla = _flash(q_mla.reshape(b, s, -1), 0, k_mla.reshape(b, s, -1), 0, v_mla.reshape(b, s, -1), 0,
                   MLA_PAD, None, MLA_HEADS, MLA_HEADS * MLA_V).reshape(m, -1)

    kb = o_fox.shape[1]
    tn_merge = 1024
    nb = d // tn_merge
    merged = _ws_matmul([(o_fox, kb, 0), (o_ml, kb, 0), (o_mla, kb, 0)],
                        [((big["w_fox_out"], l), kb, 0, 0), ((big["w_mlstm_out"], l), kb, 0, 0),
                         ((big["w_mla_out"], l), kb, 0, 0)],
                        [(gates, "tile", 0), (gates, "tile", nb), (gates, "tile", 2 * nb)],
                        _ep_merge, tm=1024, tn=tn_merge, n_out=d, out_dtype=BF16, name="merge")
    x = _ws_matmul([(merged, d, 0)], [((big["w_o"], l), d, 0, 0)], [(x, "tile", 0)], _ep_residual,
                   tm=1024, tn=512, n_out=d, out_dtype=F32, name="w_o")

    h2 = _rmsnorm(x, p["ffn_norm"])
    d_ff = big["w_gate"].shape[2]
    act = _ws_matmul([(h2, d, 0)], [((big["w_gate"], l), d, 0, 0), ((big["w_up"], l), d, 0, 0)], [], _ep_swiglu,
                     tm=2048, tn=256, n_out=d_ff, out_dtype=BF16, pairs=[(0, 0), (0, 1)], name="gate_up")
    kh = d_ff // 2
    for ki in range(2):
        x = _ws_matmul([(act, kh, ki)], [((big["w_down"], l), kh, ki, 0)], [(x, "tile", 0)], _ep_residual,
                       tm=1024, tn=512, n_out=d, out_dtype=F32, name=f"down{ki}")
    return x


def kernel(x, positions, mix_norm, w_in, fox_f_bias, fox_q_gain, fox_k_gain, mlstm_conv_w, mlstm_conv_b,
           mlstm_i_bias, mlstm_f_bias, mlstm_out_gain, mla_cq_gain, mla_ckv_gain, mla_w_uq, mla_w_ukv,
           mla_q_gain, mla_k_gain, w_fox_out, w_mlstm_out, w_mla_out, gate_bias, w_o, ffn_norm, w_gate,
           w_up, w_down):
    b, s, d = x.shape
    depth = w_in.shape[0]
    inv_freq = jnp.power(ROPE_THETA, -jnp.arange(0, MLA_ROPE, 2, dtype=F32) / MLA_ROPE)
    ang = positions.astype(F32).reshape(b * s, 1) * inv_freq
    zpad = jnp.zeros((b * s, LANES - MLA_ROPE), F32)
    cosf = jnp.concatenate([jnp.cos(ang), jnp.cos(ang), zpad], axis=1)
    sinf = jnp.concatenate([-jnp.sin(ang), jnp.sin(ang), zpad], axis=1)
    names = dict(mix_norm=mix_norm, fox_f_bias=fox_f_bias, fox_q_gain=fox_q_gain, fox_k_gain=fox_k_gain,
                 mlstm_conv_w=mlstm_conv_w, mlstm_conv_b=mlstm_conv_b, mlstm_i_bias=mlstm_i_bias,
                 mlstm_f_bias=mlstm_f_bias, mlstm_out_gain=mlstm_out_gain, mla_cq_gain=mla_cq_gain,
                 mla_ckv_gain=mla_ckv_gain, mla_w_uq=mla_w_uq, mla_w_ukv=mla_w_ukv, mla_q_gain=mla_q_gain,
                 mla_k_gain=mla_k_gain, gate_bias=gate_bias, ffn_norm=ffn_norm)
    big = dict(w_in_t=jnp.swapaxes(w_in, 1, 2), w_fox_out=w_fox_out, w_mlstm_out=w_mlstm_out, w_mla_out=w_mla_out, w_o=w_o, w_gate=w_gate,
               w_up=w_up, w_down=w_down)
    xf = x.reshape(b * s, d)
    for l in range(depth):
        xf = _layer(xf, cosf, sinf, b, s, {k: v[l] for k, v in names.items()}, big, l)
    return xf.reshape(b, s, d)
```

```python
import functools

import jax
import jax.numpy as jnp
from jax import lax
from jax.experimental import pallas as pl
from jax.experimental.pallas import tpu as pltpu

F32 = jnp.float32
BF16 = jnp.bfloat16

NORM_EPS = 1e-6
ROPE_THETA = 10000.0
LOG2E = 1.4426950408889634

FOX_HEADS = 8
FOX_DIM = 128
ML_HEADS = 8
ML_QK = 64
ML_V = 128
ML_CONV = 4
MLA_HEADS = 8
MLA_Q_RANK = 896
MLA_KV_RANK = 512
MLA_NOPE = 128
MLA_ROPE = 64
MLA_V = 128
MLA_QK = MLA_NOPE + MLA_ROPE
MLA_PAD = 256

LANES = 128
SUBLANES = 8
VMEM_LIMIT = 56 * 1024 * 1024
ML_CHUNK = 256
ATT_BLOCK = 256
PAD = 8


def _params(*sem):
    return pltpu.CompilerParams(dimension_semantics=sem, vmem_limit_bytes=VMEM_LIMIT)


def _rmsnorm_kernel(x_ref, g_ref, *rest):
    x = x_ref[...]
    ms = jnp.mean(x * x, axis=-1, keepdims=True)
    h = (x * lax.rsqrt(ms + NORM_EPS) * g_ref[...]).astype(BF16)
    if len(rest) == 1:
        rest[0][...] = h
    else:
        w_ref, o_ref, gp_ref = rest
        o_ref[...] = h
        gp_ref[...] = lax.dot_general(h, w_ref[...].astype(BF16), (((1,), (1,)), ((), ())),
                                      preferred_element_type=F32)


def _rmsnorm(x, gain, w_small_t=None, tm=256):
    m, d = x.shape
    in_specs = [pl.BlockSpec((tm, d), lambda i: (i, 0)),
                pl.BlockSpec((1, d), lambda i: (0, 0))]
    out_specs = pl.BlockSpec((tm, d), lambda i: (i, 0))
    out_shape = jax.ShapeDtypeStruct((m, d), BF16)
    args = [x, gain.reshape(1, d)]
    if w_small_t is not None:
        in_specs.append(pl.BlockSpec(w_small_t.shape, lambda i: (0, 0)))
        out_specs = [out_specs, pl.BlockSpec((tm, LANES), lambda i: (i, 0))]
        out_shape = [out_shape, jax.ShapeDtypeStruct((m, LANES), F32)]
        args.append(w_small_t)
    return pl.pallas_call(
        _rmsnorm_kernel,
        grid=(m // tm,),
        in_specs=in_specs,
        out_specs=out_specs,
        out_shape=out_shape,
        compiler_params=_params("parallel"),
        name="rmsnorm",
    )(*args)


def _ws_kernel(*refs, nx, nw, ne, pairs, epilogue, cast_rows, wt, panel_src):
    x_refs = refs[:nx]
    w_hbm = refs[nx:nx + nw]
    e_refs = refs[nx + nw:nx + nw + ne]
    o_ref = refs[nx + nw + ne]
    scratch = refs[nx + nw + ne + 1:]
    stage, w_bf, sem = scratch[:nw], scratch[nw:2 * nw], scratch[2 * nw]
    n = pl.program_id(0)
    n_panels = pl.num_programs(0)

    def panel_copy(wi, panel):
        return pltpu.make_async_copy(panel_src[wi](w_hbm[wi], panel), stage[wi], sem.at[wi])

    @pl.when(pl.program_id(1) == 0)
    def _():
        @pl.when(n == 0)
        def _():
            for wi in range(nw):
                panel_copy(wi, 0).start()

        for wi in range(nw):
            panel_copy(wi, n).wait()

            def body(i, c, wi=wi):
                r = pl.multiple_of(i * cast_rows, cast_rows)
                w_bf[wi][pl.ds(r, cast_rows), :] = stage[wi][pl.ds(r, cast_rows), :].astype(BF16)
                return c

            lax.fori_loop(0, stage[wi].shape[0] // cast_rows, body, 0)

        @pl.when(n + 1 < n_panels)
        def _():
            for wi in range(nw):
                panel_copy(wi, n + 1).start()

    dims = (((1,), (1 if wt else 0,)), ((), ()))
    dots = [lax.dot_general(x_refs[xi][...], w_bf[wi][...], dims, preferred_element_type=F32) for xi, wi in pairs]
    o_ref[...] = epilogue(dots, [e[...] for e in e_refs]).astype(o_ref.dtype)


def _ws_matmul(xs, ws, extras, epilogue, *, tm, tn, n_out, out_dtype, pairs=None, wt=False, first_row=None, name):
    m = xs[0][0].shape[0]
    assert n_out % tn == 0 and m % tm == 0
    pairs = pairs or [(i, i) for i in range(len(ws))]
    wt = wt or first_row is not None
    in_specs = []
    args = []
    panel_src = []
    for a, kb, ki in xs:
        in_specs.append(pl.BlockSpec((tm, kb), lambda n, i, ki=ki: (i, ki)))
        args.append(a)
    for a, kb, ki, off in ws:
        layer = None
        if isinstance(a, tuple):
            a, layer = a
        assert a.dtype == F32
        if first_row is not None:
            assert len(ws) == 1 and kb == a.shape[2] and first_row % SUBLANES == 0 and tn % SUBLANES == 0
            assert first_row + n_out <= a.shape[1]
            panel_src.append(lambda w, p, layer=layer: w.at[
                layer, pl.ds(pl.multiple_of(first_row + p * tn, SUBLANES), tn), :])
        elif wt:
            assert layer is None and kb == a.shape[1]
            panel_src.append(lambda w, p: w.at[pl.ds(pl.multiple_of(p * tn, SUBLANES), tn), :])
        else:
            assert layer is not None
            panel_src.append(lambda w, p, layer=layer, kb=kb, ki=ki, off=off: w.at[
                layer, pl.ds(ki * kb, kb), pl.ds(pl.multiple_of((p + off) * tn, LANES), tn)])
        in_specs.append(pl.BlockSpec(memory_space=pl.ANY))
        args.append(a)
    for a, kind, off in extras:
        if kind == "tile":
            in_specs.append(pl.BlockSpec((tm, tn), lambda n, i, off=off: (i, n + off)))
        else:
            in_specs.append(pl.BlockSpec((1, tn), lambda n, i, off=off: (0, n + off)))
        args.append(a)
    panel_shapes = [(tn, kb) if wt else (kb, tn) for _, kb, _, _ in ws]
    cast_rows = LANES
    for shp in panel_shapes:
        assert shp[0] % cast_rows == 0, shp
    scratch = ([pltpu.VMEM(shp, F32) for shp in panel_shapes] + [pltpu.VMEM(shp, BF16) for shp in panel_shapes]
               + [pltpu.SemaphoreType.DMA((len(ws),))])
    kern = functools.partial(_ws_kernel, nx=len(xs), nw=len(ws), ne=len(extras), pairs=pairs, epilogue=epilogue,
                             cast_rows=cast_rows, wt=wt, panel_src=panel_src)
    return pl.pallas_call(
        kern,
        grid=(n_out // tn, m // tm),
        in_specs=in_specs,
        out_specs=pl.BlockSpec((tm, tn), lambda n, i: (i, n)),
        out_shape=jax.ShapeDtypeStruct((m, n_out), out_dtype),
        scratch_shapes=scratch,
        compiler_params=_params("arbitrary", "arbitrary"),
        name=name,
    )(*args)


def _ep_plain(dots, extras):
    return dots[0]


def _ep_bias_sigmoid(dots, extras):
    return jax.nn.sigmoid(dots[0] + extras[0])


def _ep_residual(dots, extras):
    return extras[0] + dots[0]


def _ep_swiglu(dots, extras):
    g, u = dots
    return g * jax.nn.sigmoid(g) * u


def _ep_merge(dots, extras):
    return (extras[0].astype(F32) * dots[0] + extras[1].astype(F32) * dots[1]
            + extras[2].astype(F32) * dots[2])


def _log_sigmoid(x):
    return jnp.minimum(x, 0.0) - jnp.log1p(jnp.exp(-jnp.abs(x)))


def _cumsum_lanes(x, seg):
    n = x.shape[-1]
    pos = lax.broadcasted_iota(jnp.int32, x.shape, 1) % seg
    shift = 1
    while shift < seg:
        x = x + jnp.where(pos >= shift, pltpu.roll(x, shift, axis=1), 0.0)
        shift *= 2
    del n
    return x


def _gates_kernel(g_ref, bias_ref, rows_ref, *, chunk):
    g = g_ref[...] + bias_ref[...]
    gt = g.T
    s = gt.shape[1]
    rows_ref[0:8, :] = _cumsum_lanes(_log_sigmoid(gt[64:72, :]), s) * LOG2E
    rows_ref[8:16, :] = gt[72:80, :]
    rows_ref[16:24, :] = _cumsum_lanes(_log_sigmoid(gt[80:88, :]), chunk)


def _gate_rows(g_pre, bias_vec, b, s):
    return pl.pallas_call(
        functools.partial(_gates_kernel, chunk=ML_CHUNK),
        grid=(b,),
        in_specs=[pl.BlockSpec((s, LANES), lambda i: (i, 0)),
                  pl.BlockSpec((1, LANES), lambda i: (0, 0))],
        out_specs=pl.BlockSpec((None, 24, s), lambda i: (i, 0, 0)),
        out_shape=jax.ShapeDtypeStruct((b, 24, s), F32),
        compiler_params=_params("parallel"),
        name="gate_rows",
    )(g_pre, bias_vec)


def _flash_kernel(*refs, blk, has_bias):
    if has_bias:
        q_ref, kraw_ref, v_ref, d_ref, qg_ref, kg_ref, o_ref, k_ref = refs
    else:
        q_ref, k_ref, v_ref, o_ref = refs
        d_ref = None
    h = pl.program_id(1)
    s_len = k_ref.shape[0]
    row = lax.broadcasted_iota(jnp.int32, (blk, blk), 0)
    col = lax.broadcasted_iota(jnp.int32, (blk, blk), 1)
    nt = (((1,), (1,)), ((), ()))

    def qk_norm(x, gain):
        xf = x.astype(F32)
        return xf * lax.rsqrt(jnp.mean(xf * xf, axis=-1, keepdims=True) + NORM_EPS) * gain

    if has_bias:
        for c in range(s_len // blk):
            k_ref[c * blk:(c + 1) * blk, :] = qk_norm(kraw_ref[c * blk:(c + 1) * blk, :], kg_ref[...]).astype(BF16)
    for qi in range(s_len // blk):
        lo = qi * blk
        q = q_ref[lo:lo + blk, :]
        if has_bias:
            q = (qk_norm(q, qg_ref[...]) * (FOX_DIM ** -0.5 * LOG2E)).astype(BF16)
        s_diag = lax.dot_general(q, k_ref[lo:lo + blk, :], nt, preferred_element_type=F32)
        if has_bias:
            s_diag = s_diag - d_ref[pl.ds(h, 1), lo:lo + blk]
        s_diag = jnp.where(col <= row, s_diag, -jnp.inf)
        m = jnp.max(s_diag, axis=-1, keepdims=True)
        if qi:
            s_off = lax.dot_general(q, k_ref[0:lo, :], nt, preferred_element_type=F32)
            if has_bias:
                s_off = s_off - d_ref[pl.ds(h, 1), 0:lo]
            m = jnp.maximum(m, jnp.max(s_off, axis=-1, keepdims=True))
        p_diag = jnp.exp2(s_diag - m)
        l = jnp.sum(p_diag, axis=-1, keepdims=True)
        acc = jnp.dot(p_diag.astype(BF16), v_ref[lo:lo + blk, :], preferred_element_type=F32)
        if qi:
            p_off = jnp.exp2(s_off - m)
            l = l + jnp.sum(p_off, axis=-1, keepdims=True)
            acc = acc + jnp.dot(p_off.astype(BF16), v_ref[0:lo, :], preferred_element_type=F32)
        o_ref[lo:lo + blk, :] = (acc / l).astype(o_ref.dtype)


def _flash(q_arr, q_off, k_arr, k_off, v_arr, v_off, dk, n_heads, out_width, fox=None):
    b, s, _ = q_arr.shape
    blk = ATT_BLOCK
    in_specs = [
        pl.BlockSpec((None, s, dk), lambda bi, h: (bi, 0, q_off + h)),
        pl.BlockSpec((None, s, dk), lambda bi, h: (bi, 0, k_off + h)),
        pl.BlockSpec((None, s, LANES), lambda bi, h: (bi, 0, v_off + h)),
    ]
    args = [q_arr, k_arr, v_arr]
    scratch = []
    if fox is not None:
        rows, q_gain, k_gain = fox
        in_specs += [pl.BlockSpec((None, 8, s), lambda bi, h: (bi, 0, 0)),
                     pl.BlockSpec((1, dk), lambda bi, h: (0, 0)),
                     pl.BlockSpec((1, dk), lambda bi, h: (0, 0))]
        args += [rows, q_gain.reshape(1, dk), k_gain.reshape(1, dk)]
        scratch = [pltpu.VMEM((s, dk), BF16)]
    return pl.pallas_call(
        functools.partial(_flash_kernel, blk=blk, has_bias=fox is not None),
        grid=(b, n_heads),
        in_specs=in_specs,
        out_specs=pl.BlockSpec((None, s, LANES), lambda bi, h: (bi, 0, h)),
        out_shape=jax.ShapeDtypeStruct((b, s, out_width), BF16),
        scratch_shapes=scratch,
        compiler_params=_params("parallel", "parallel"),
        name="flash_bias" if fox is not None else "flash",
    )(*args)


def _mlstm_kernel(q_ref, k_ref, v_ref, og_ref, rows_ref, gain_ref, wq_ref, wk_ref, bq_ref, bk_ref, o_ref,
                  xq_ref, xk_ref, *, chunk):
    L = chunk
    hp = pl.program_id(1)
    s = q_ref.shape[0]
    nc = s // L
    for x_ref, xs_ref in ((q_ref, xq_ref), (k_ref, xk_ref)):
        xs_ref[0:PAD, :] = jnp.zeros((PAD, LANES), F32)
        for c in range(nc):
            xs_ref[PAD + c * L:PAD + (c + 1) * L, :] = x_ref[c * L:(c + 1) * L, :].astype(F32)

    def conv_silu(xs_ref, w_ref, b_ref, t0):
        y = b_ref[...]
        for j in range(ML_CONV):
            off = PAD + t0 - (ML_CONV - 1) + j
            y = y + w_ref[j:j + 1, :] * xs_ref[off:off + L, :]
        return y * jax.nn.sigmoid(y)

    lane = lax.broadcasted_iota(jnp.int32, (1, LANES), 1)
    row = lax.broadcasted_iota(jnp.int32, (L, L), 0)
    col = lax.broadcasted_iota(jnp.int32, (L, L), 1)
    causal = col <= row
    eye = col == row
    col1 = lax.broadcasted_iota(jnp.int32, (1, L), 1)
    row1 = lax.broadcasted_iota(jnp.int32, (L, 1), 0)

    def to_col(r):
        return jnp.sum(jnp.where(eye, r, 0.0), axis=-1, keepdims=True)

    def head_step(t0, hh, qp, kp, kp_b, state):
        c_state, n_state, m_prev = state
        head = 2 * hp + hh
        li = rows_ref[pl.ds(8 + head, 1), pl.ds(t0, L)]
        bc = rows_ref[pl.ds(16 + head, 1), pl.ds(t0, L)]
        a_row = li - bc
        bc_col = to_col(bc)
        log_w = jnp.where(causal, bc_col + a_row, -jnp.inf)
        log_inter = bc_col + m_prev
        m_t = jnp.maximum(log_inter, jnp.max(log_w, axis=-1, keepdims=True))
        w = jnp.exp(log_w - m_t)
        inter = jnp.exp(log_inter - m_t)
        head_lanes = (lane >= hh * ML_QK) & (lane < (hh + 1) * ML_QK)
        qh = jnp.where(head_lanes, qp, jnp.zeros_like(qp))
        qk = lax.dot_general(qh, kp_b, (((1,), (1,)), ((), ())), preferred_element_type=F32) * w
        vh = v_ref[pl.ds(t0, L), hh * ML_V:(hh + 1) * ML_V]
        num = (jnp.dot(qk.astype(BF16), vh, preferred_element_type=F32)
               + inter * jnp.dot(qh, c_state.astype(BF16), preferred_element_type=F32))
        den = (jnp.sum(qk, axis=-1, keepdims=True)
               + inter * jnp.sum(qh.astype(F32) * n_state, axis=-1, keepdims=True))
        h_out = num / jnp.maximum(jnp.abs(den), jnp.exp(-m_t))
        hn = h_out * lax.rsqrt(jnp.mean(h_out * h_out, axis=-1, keepdims=True) + NORM_EPS)
        hn = hn * gain_ref[:, hh * ML_V:(hh + 1) * ML_V]
        og = og_ref[pl.ds(t0, L), hh * ML_V:(hh + 1) * ML_V].astype(F32)
        o_ref[pl.ds(t0, L), hh * ML_V:(hh + 1) * ML_V] = (hn * jax.nn.sigmoid(og)).astype(o_ref.dtype)
        m_new = jnp.sum(jnp.where(row1 == L - 1, m_t, 0.0), axis=0, keepdims=True)
        b_last = jnp.sum(jnp.where(col1 == L - 1, bc, 0.0), axis=-1, keepdims=True)
        decay = jnp.exp(b_last + m_prev - m_new)
        w_last = to_col(jnp.exp(b_last + a_row - m_new))
        kw = kp * w_last
        c_upd = lax.dot_general(kw.astype(BF16), vh, (((0,), (0,)), ((), ())), preferred_element_type=F32)
        return decay * c_state + c_upd, decay * n_state + jnp.sum(kw, axis=0, keepdims=True), m_new

    states = [(jnp.zeros((LANES, ML_V), F32), jnp.zeros((1, LANES), F32), jnp.zeros((1, 1), F32))
              for _ in range(2)]
    for ci in range(nc):
        t0 = ci * L
        qp = conv_silu(xq_ref, wq_ref, bq_ref, t0).astype(BF16)
        kp = conv_silu(xk_ref, wk_ref, bk_ref, t0) * (ML_QK ** -0.5)
        kp_b = kp.astype(BF16)
        states = [head_step(t0, hh, qp, kp, kp_b, states[hh]) for hh in range(2)]


def _mlstm(qkv_arr, og_arr, rows, out_gain, conv_w, conv_b):
    b, s, _ = qkv_arr.shape
    npair = ML_HEADS // 2
    v_blk0 = 2 * ML_HEADS * ML_QK // (2 * ML_V)
    pair = lambda off: (lambda bi, hp: (0, off + hp))
    return pl.pallas_call(
        functools.partial(_mlstm_kernel, chunk=ML_CHUNK),
        grid=(b, npair),
        in_specs=[pl.BlockSpec((None, s, LANES), lambda bi, hp: (bi, 0, hp)),
                  pl.BlockSpec((None, s, LANES), lambda bi, hp: (bi, 0, npair + hp)),
                  pl.BlockSpec((None, s, 2 * ML_V), lambda bi, hp: (bi, 0, v_blk0 + hp)),
                  pl.BlockSpec((None, s, 2 * ML_V), lambda bi, hp: (bi, 0, hp)),
                  pl.BlockSpec((None, 24, s), lambda bi, hp: (bi, 0, 0)),
                  pl.BlockSpec((1, 2 * ML_V), pair(0)),
                  pl.BlockSpec((ML_CONV, LANES), pair(0)),
                  pl.BlockSpec((ML_CONV, LANES), pair(npair)),
                  pl.BlockSpec((1, LANES), pair(0)),
                  pl.BlockSpec((1, LANES), pair(npair))],
        out_specs=pl.BlockSpec((None, s, 2 * ML_V), lambda bi, hp: (bi, 0, hp)),
        out_shape=jax.ShapeDtypeStruct((b, s, ML_HEADS * ML_V), BF16),
        scratch_shapes=[pltpu.VMEM((s + PAD, LANES), F32), pltpu.VMEM((s + PAD, LANES), F32)],
        compiler_params=_params("parallel", "parallel"),
        name="mlstm",
    )(qkv_arr, qkv_arr, qkv_arr, og_arr, rows, out_gain.reshape(1, -1), conv_w, conv_w,
      conv_b.reshape(1, -1), conv_b.reshape(1, -1))


def _mla_prep_kernel(s_ref, cos_ref, sin_ref, wuq_ref, wukv_ref, cqg_ref, ckvg_ref, qgn_ref, qgr_ref,
                     kgn_ref, kgr_ref, q_ref, k_ref, v_ref):
    scale = MLA_QK ** -0.5 * LOG2E
    lane = lax.broadcasted_iota(jnp.int32, (1, LANES), 1)
    cosf = cos_ref[...]
    sinf = sin_ref[...]

    def norm(x, g):
        return x * lax.rsqrt(jnp.mean(x * x, axis=-1, keepdims=True) + NORM_EPS) * g

    def rope(x):
        swap = jnp.where(lane < MLA_ROPE // 2, pltpu.roll(x, LANES - MLA_ROPE // 2, axis=1),
                         pltpu.roll(x, MLA_ROPE // 2, axis=1))
        return x * cosf + swap * sinf

    cq = s_ref[:, 0:MLA_Q_RANK]
    ckv = s_ref[:, MLA_Q_RANK:MLA_Q_RANK + MLA_KV_RANK]
    kr = jnp.where(lane < MLA_ROPE, s_ref[:, MLA_Q_RANK + MLA_KV_RANK:], 0.0)
    qf = jnp.dot(norm(cq, cqg_ref[...]).astype(BF16), wuq_ref[...], preferred_element_type=F32)
    kvf = jnp.dot(norm(ckv, ckvg_ref[...]).astype(BF16), wukv_ref[...], preferred_element_type=F32)
    ss_kr = jnp.sum(kr * kr, axis=-1, keepdims=True)
    kr_g = kr * kgr_ref[...]
    nw = MLA_HEADS * MLA_NOPE
    for h in range(MLA_HEADS):
        qn = qf[:, h * MLA_NOPE:(h + 1) * MLA_NOPE]
        qr = qf[:, nw + h * LANES:nw + (h + 1) * LANES]
        ss = jnp.sum(qn * qn, axis=-1, keepdims=True) + jnp.sum(qr * qr, axis=-1, keepdims=True)
        r = lax.rsqrt(ss / MLA_QK + NORM_EPS)
        q_ref[:, h * MLA_PAD:h * MLA_PAD + MLA_NOPE] = (qn * r * qgn_ref[...] * scale).astype(q_ref.dtype)
        q_ref[:, h * MLA_PAD + MLA_NOPE:(h + 1) * MLA_PAD] = (rope(qr * r * qgr_ref[...]) * scale).astype(q_ref.dtype)
        kn = kvf[:, h * 2 * MLA_NOPE:h * 2 * MLA_NOPE + MLA_NOPE]
        rk = lax.rsqrt((jnp.sum(kn * kn, axis=-1, keepdims=True) + ss_kr) / MLA_QK + NORM_EPS)
        k_ref[:, h * MLA_PAD:h * MLA_PAD + MLA_NOPE] = (kn * rk * kgn_ref[...]).astype(k_ref.dtype)
        k_ref[:, h * MLA_PAD + MLA_NOPE:(h + 1) * MLA_PAD] = rope(kr_g * rk).astype(k_ref.dtype)
        v_ref[:, h * MLA_V:(h + 1) * MLA_V] = kvf[:, h * 2 * MLA_NOPE + MLA_NOPE:(h + 1) * 2 * MLA_NOPE].astype(v_ref.dtype)


def _mla_prep(seg3, cosf, sinf, wuq_p, wukv_b, cq_gain, ckv_gain, q_gain, k_gain, tm=256):
    m, w3 = seg3.shape
    pad = jnp.zeros((LANES - MLA_ROPE,), F32)
    qgn = q_gain[:MLA_NOPE].reshape(1, -1)
    qgr = jnp.concatenate([q_gain[MLA_NOPE:], pad]).reshape(1, -1)
    kgn = k_gain[:MLA_NOPE].reshape(1, -1)
    kgr = jnp.concatenate([k_gain[MLA_NOPE:], pad]).reshape(1, -1)
    const = lambda i: (0, 0)
    return pl.pallas_call(
        _mla_prep_kernel,
        grid=(m // tm,),
        in_specs=[pl.BlockSpec((tm, w3), lambda i: (i, 0)),
                  pl.BlockSpec((tm, LANES), lambda i: (i, 0)),
                  pl.BlockSpec((tm, LANES), lambda i: (i, 0)),
                  pl.BlockSpec(wuq_p.shape, const),
                  pl.BlockSpec(wukv_b.shape, const),
                  pl.BlockSpec((1, MLA_Q_RANK), const),
                  pl.BlockSpec((1, MLA_KV_RANK), const),
                  pl.BlockSpec((1, LANES), const),
                  pl.BlockSpec((1, LANES), const),
                  pl.BlockSpec((1, LANES), const),
                  pl.BlockSpec((1, LANES), const)],
        out_specs=[pl.BlockSpec((tm, MLA_HEADS * MLA_PAD), lambda i: (i, 0)),
                   pl.BlockSpec((tm, MLA_HEADS * MLA_PAD), lambda i: (i, 0)),
                   pl.BlockSpec((tm, MLA_HEADS * MLA_V), lambda i: (i, 0))],
        out_shape=[jax.ShapeDtypeStruct((m, MLA_HEADS * MLA_PAD), BF16),
                   jax.ShapeDtypeStruct((m, MLA_HEADS * MLA_PAD), BF16),
                   jax.ShapeDtypeStruct((m, MLA_HEADS * MLA_V), BF16)],
        compiler_params=_params("parallel"),
        name="mla_prep",
    )(seg3, cosf, sinf, wuq_p, wukv_b, cq_gain.reshape(1, -1), ckv_gain.reshape(1, -1), qgn, qgr, kgn, kgr)


_FW = FOX_HEADS * FOX_DIM
_IN_SIZES = (_FW, _FW, _FW, FOX_HEADS, ML_HEADS * ML_QK, ML_HEADS * ML_QK, ML_HEADS * ML_V,
             ML_HEADS, ML_HEADS, ML_HEADS * ML_V, MLA_Q_RANK, MLA_KV_RANK, MLA_ROPE)
_IN_OFFS = [sum(_IN_SIZES[:i]) for i in range(len(_IN_SIZES) + 1)]
(_C_FQ, _, _, _C_FF, _C_MQ, _, _, _C_MI, _C_MF, _C_MO, _C_CQ, _, _, _C_GATES) = _IN_OFFS


def _small_gate_weight(w_in_t, l):
    d = w_in_t.shape[2]
    rows = [jnp.zeros((MLA_ROPE, d), w_in_t.dtype),
            w_in_t[l, _C_FF:_C_FF + FOX_HEADS],
            w_in_t[l, _C_MI:_C_MI + ML_HEADS],
            w_in_t[l, _C_MF:_C_MF + ML_HEADS],
            jnp.zeros((LANES - MLA_ROPE - FOX_HEADS - 2 * ML_HEADS, d), w_in_t.dtype)]
    return jnp.concatenate(rows, axis=0)


def _layer(x, cosf, sinf, b, s, p, big, l):
    m, d = x.shape
    h1, g_pre = _rmsnorm(x, p["mix_norm"], _small_gate_weight(big["w_in_t"], l))
    w_in = (big["w_in_t"], l)

    def in_proj(first_col, width, out_dtype, name, extras=(), epilogue=_ep_plain, tn=1024):
        return _ws_matmul([(h1, d, 0)], [(w_in, d, 0, 0)], list(extras), epilogue, tm=1024, tn=tn, n_out=width,
                          out_dtype=out_dtype, first_row=first_col, name=name)

    proj_fox = in_proj(_C_FQ, 3 * _FW, BF16, "proj_fox")
    proj_ml = in_proj(_C_MQ, 2 * ML_HEADS * ML_QK + ML_HEADS * ML_V, BF16, "proj_ml")
    proj_mo = in_proj(_C_MO, ML_HEADS * ML_V, BF16, "proj_mo")
    w_mla = MLA_Q_RANK + MLA_KV_RANK + LANES
    seg3 = in_proj(_C_CQ, w_mla, F32, "proj_mla", tn=w_mla // 2)
    gates = in_proj(_C_GATES, 3 * d, BF16, "gates", [(p["gate_bias"].reshape(1, -1), "row", 0)], _ep_bias_sigmoid)

    bias_vec = jnp.concatenate([jnp.zeros((MLA_ROPE,), F32), p["fox_f_bias"], p["mlstm_i_bias"], p["mlstm_f_bias"],
                                jnp.zeros((LANES - MLA_ROPE - 24,), F32)]).reshape(1, LANES)
    rows = _gate_rows(g_pre, bias_vec, b, s)

    proj_fox_3d = proj_fox.reshape(b, s, -1)
    o_fox = _flash(proj_fox_3d, 0, proj_fox_3d, FOX_HEADS, proj_fox_3d, 2 * FOX_HEADS, FOX_DIM, FOX_HEADS,
                   FOX_HEADS * FOX_DIM, fox=(rows, p["fox_q_gain"], p["fox_k_gain"])).reshape(m, -1)
    o_ml = _mlstm(proj_ml.reshape(b, s, -1), proj_mo.reshape(b, s, -1), rows, p["mlstm_out_gain"],
                  p["mlstm_conv_w"], p["mlstm_conv_b"]).reshape(m, -1)
    wuq = p["mla_w_uq"].reshape(MLA_Q_RANK, MLA_HEADS, MLA_QK)
    wuq_p = jnp.concatenate([
        wuq[:, :, :MLA_NOPE].reshape(MLA_Q_RANK, -1),
        jnp.pad(wuq[:, :, MLA_NOPE:], ((0, 0), (0, 0), (0, LANES - MLA_ROPE))).reshape(MLA_Q_RANK, -1)],
        axis=1).astype(BF16)
    q_mla, k_mla, v_mla = _mla_prep(seg3, cosf, sinf, wuq_p, p["mla_w_ukv"].astype(BF16), p["mla_cq_gain"],
                                    p["mla_ckv_gain"], p["mla_q_gain"], p["mla_k_gain"])
    o_mla = _flash(q_mla.reshape(b, s, -1), 0, k_mla.reshape(b, s, -1), 0, v_mla.reshape(b, s, -1), 0,
                   MLA_PAD, MLA_HEADS, MLA_HEADS * MLA_V).reshape(m, -1)

    kb = o_fox.shape[1]
    tn_merge = 1024
    nb = d // tn_merge
    merged = _ws_matmul([(o_fox, kb, 0), (o_ml, kb, 0), (o_mla, kb, 0)],
                        [((big["w_fox_out"], l), kb, 0, 0), ((big["w_mlstm_out"], l), kb, 0, 0),
                         ((big["w_mla_out"], l), kb, 0, 0)],
                        [(gates, "tile", 0), (gates, "tile", nb), (gates, "tile", 2 * nb)],
                        _ep_merge, tm=1024, tn=tn_merge, n_out=d, out_dtype=BF16, name="merge")
    x = _ws_matmul([(merged, d, 0)], [((big["w_o"], l), d, 0, 0)], [(x, "tile", 0)], _ep_residual,
                   tm=1024, tn=512, n_out=d, out_dtype=F32, name="w_o")

    h2 = _rmsnorm(x, p["ffn_norm"])
    d_ff = big["w_gate"].shape[2]
    act = _ws_matmul([(h2, d, 0)], [((big["w_gate"], l), d, 0, 0), ((big["w_up"], l), d, 0, 0)], [], _ep_swiglu,
                     tm=2048, tn=256, n_out=d_ff, out_dtype=BF16, pairs=[(0, 0), (0, 1)], name="gate_up")
    kh = d_ff // 2
    for ki in range(2):
        x = _ws_matmul([(act, kh, ki)], [((big["w_down"], l), kh, ki, 0)], [(x, "tile", 0)], _ep_residual,
                       tm=1024, tn=512, n_out=d, out_dtype=F32, name=f"down{ki}")
    return x


def kernel(x, positions, mix_norm, w_in, fox_f_bias, fox_q_gain, fox_k_gain, mlstm_conv_w, mlstm_conv_b,
           mlstm_i_bias, mlstm_f_bias, mlstm_out_gain, mla_cq_gain, mla_ckv_gain, mla_w_uq, mla_w_ukv,
           mla_q_gain, mla_k_gain, w_fox_out, w_mlstm_out, w_mla_out, gate_bias, w_o, ffn_norm, w_gate,
           w_up, w_down):
    b, s, d = x.shape
    depth = w_in.shape[0]
    inv_freq = jnp.power(ROPE_THETA, -jnp.arange(0, MLA_ROPE, 2, dtype=F32) / MLA_ROPE)
    ang = positions.astype(F32).reshape(b * s, 1) * inv_freq
    zpad = jnp.zeros((b * s, LANES - MLA_ROPE), F32)
    cosf = jnp.concatenate([jnp.cos(ang), jnp.cos(ang), zpad], axis=1)
    sinf = jnp.concatenate([-jnp.sin(ang), jnp.sin(ang), zpad], axis=1)
    names = dict(mix_norm=mix_norm, fox_f_bias=fox_f_bias, fox_q_gain=fox_q_gain, fox_k_gain=fox_k_gain,
                 mlstm_conv_w=mlstm_conv_w, mlstm_conv_b=mlstm_conv_b, mlstm_i_bias=mlstm_i_bias,
                 mlstm_f_bias=mlstm_f_bias, mlstm_out_gain=mlstm_out_gain, mla_cq_gain=mla_cq_gain,
                 mla_ckv_gain=mla_ckv_gain, mla_w_uq=mla_w_uq, mla_w_ukv=mla_w_ukv, mla_q_gain=mla_q_gain,
                 mla_k_gain=mla_k_gain, gate_bias=gate_bias, ffn_norm=ffn_norm)
    big = dict(w_in_t=jnp.swapaxes(w_in, 1, 2), w_fox_out=w_fox_out, w_mlstm_out=w_mlstm_out, w_mla_out=w_mla_out, w_o=w_o, w_gate=w_gate,
               w_up=w_up, w_down=w_down)
    xf = x.reshape(b * s, d)
    for l in range(depth):
        xf = _layer(xf, cosf, sinf, b, s, {k: v[l] for k, v in names.items()}, big, l)
    return xf.reshape(b, s, d)
```

```python
import functools

import jax
import jax.numpy as jnp
from jax import lax
from jax.experimental import pallas as pl
from jax.experimental.pallas import tpu as pltpu

F32 = jnp.float32
BF16 = jnp.bfloat16

NORM_EPS = 1e-6
ROPE_THETA = 10000.0
LOG2E = 1.4426950408889634

FOX_HEADS = 8
FOX_DIM = 128
ML_HEADS = 8
ML_QK = 64
ML_V = 128
ML_CONV = 4
MLA_HEADS = 8
MLA_Q_RANK = 896
MLA_KV_RANK = 512
MLA_NOPE = 128
MLA_ROPE = 64
MLA_V = 128
MLA_QK = MLA_NOPE + MLA_ROPE
MLA_PAD = 256

LANES = 128
SUBLANES = 8
VMEM_LIMIT = 56 * 1024 * 1024
ML_CHUNK = 256
ATT_BLOCK = 256
FIRST_PANEL_CHUNKS = 4
PAD = 8


def _params(*sem):
    return pltpu.CompilerParams(dimension_semantics=sem, vmem_limit_bytes=VMEM_LIMIT)


def _rmsnorm_kernel(x_ref, g_ref, *rest):
    x = x_ref[...]
    ms = jnp.mean(x * x, axis=-1, keepdims=True)
    h = (x * lax.rsqrt(ms + NORM_EPS) * g_ref[...]).astype(BF16)
    if len(rest) == 1:
        rest[0][...] = h
    else:
        w_ref, o_ref, gp_ref = rest
        o_ref[...] = h
        gp_ref[...] = lax.dot_general(h, w_ref[...].astype(BF16), (((1,), (1,)), ((), ())),
                                      preferred_element_type=F32)


def _rmsnorm(x, gain, w_small_t=None, tm=256):
    m, d = x.shape
    in_specs = [pl.BlockSpec((tm, d), lambda i: (i, 0)),
                pl.BlockSpec((1, d), lambda i: (0, 0))]
    out_specs = pl.BlockSpec((tm, d), lambda i: (i, 0))
    out_shape = jax.ShapeDtypeStruct((m, d), BF16)
    args = [x, gain.reshape(1, d)]
    if w_small_t is not None:
        in_specs.append(pl.BlockSpec(w_small_t.shape, lambda i: (0, 0)))
        out_specs = [out_specs, pl.BlockSpec((tm, LANES), lambda i: (i, 0))]
        out_shape = [out_shape, jax.ShapeDtypeStruct((m, LANES), F32)]
        args.append(w_small_t)
    return pl.pallas_call(
        _rmsnorm_kernel,
        grid=(m // tm,),
        in_specs=in_specs,
        out_specs=out_specs,
        out_shape=out_shape,
        compiler_params=_params("parallel"),
        name="rmsnorm",
    )(*args)


def _ws_kernel(*refs, nx, nw, ne, pairs, epilogue, cast_rows, wt, panel_src):
    x_refs = refs[:nx]
    w_hbm = refs[nx:nx + nw]
    e_refs = refs[nx + nw:nx + nw + ne]
    o_ref = refs[nx + nw + ne]
    scratch = refs[nx + nw + ne + 1:]
    stage, w_bf, sem = scratch[:nw], scratch[nw:2 * nw], scratch[2 * nw]
    n = pl.program_id(0)
    n_panels = pl.num_programs(0)

    def panel_copy(wi, panel):
        return pltpu.make_async_copy(panel_src[wi](w_hbm[wi], panel), stage[wi], sem.at[wi, 0])

    def row_chunks(wi):
        nblk = stage[wi].shape[0] // cast_rows
        k = min(FIRST_PANEL_CHUNKS, nblk)
        cuts = [(nblk * c) // k * cast_rows for c in range(k + 1)]
        return [(cuts[c], cuts[c + 1] - cuts[c]) for c in range(k)]

    def chunk_copy(wi, c):
        r0, nr = row_chunks(wi)[c]
        src = panel_src[wi](w_hbm[wi], 0)
        return pltpu.make_async_copy(src.at[pl.ds(r0, nr), :], stage[wi].at[pl.ds(r0, nr), :], sem.at[wi, c])

    def cast(wi, r0, nr):
        def body(i, c):
            r = pl.multiple_of(r0 + i * cast_rows, cast_rows)
            w_bf[wi][pl.ds(r, cast_rows), :] = stage[wi][pl.ds(r, cast_rows), :].astype(BF16)
            return c

        lax.fori_loop(0, nr // cast_rows, body, 0)

    @pl.when(pl.program_id(1) == 0)
    def _():
        @pl.when(n == 0)
        def _():
            for wi in range(nw):
                for c in range(len(row_chunks(wi))):
                    chunk_copy(wi, c).start()
            for wi in range(nw):
                for c, (r0, nr) in enumerate(row_chunks(wi)):
                    chunk_copy(wi, c).wait()
                    cast(wi, r0, nr)

        @pl.when(n > 0)
        def _():
            for wi in range(nw):
                panel_copy(wi, n).wait()
                cast(wi, 0, stage[wi].shape[0])

        @pl.when(n + 1 < n_panels)
        def _():
            for wi in range(nw):
                panel_copy(wi, n + 1).start()

    dims = (((1,), (1 if wt else 0,)), ((), ()))
    dots = [lax.dot_general(x_refs[xi][...], w_bf[wi][...], dims, preferred_element_type=F32) for xi, wi in pairs]
    o_ref[...] = epilogue(dots, [e[...] for e in e_refs]).astype(o_ref.dtype)


def _ws_matmul(xs, ws, extras, epilogue, *, tm, tn, n_out, out_dtype, pairs=None, wt=False, first_row=None, name):
    m = xs[0][0].shape[0]
    assert n_out % tn == 0 and m % tm == 0
    pairs = pairs or [(i, i) for i in range(len(ws))]
    wt = wt or first_row is not None
    in_specs = []
    args = []
    panel_src = []
    for a, kb, ki in xs:
        in_specs.append(pl.BlockSpec((tm, kb), lambda n, i, ki=ki: (i, ki)))
        args.append(a)
    for a, kb, ki, off in ws:
        layer = None
        if isinstance(a, tuple):
            a, layer = a
        assert a.dtype == F32
        if first_row is not None:
            assert len(ws) == 1 and kb == a.shape[2] and first_row % SUBLANES == 0 and tn % SUBLANES == 0
            assert first_row + n_out <= a.shape[1]
            panel_src.append(lambda w, p, layer=layer: w.at[
                layer, pl.ds(pl.multiple_of(first_row + p * tn, SUBLANES), tn), :])
        elif wt:
            assert layer is None and kb == a.shape[1]
            panel_src.append(lambda w, p: w.at[pl.ds(pl.multiple_of(p * tn, SUBLANES), tn), :])
        else:
            assert layer is not None
            panel_src.append(lambda w, p, layer=layer, kb=kb, ki=ki, off=off: w.at[
                layer, pl.ds(ki * kb, kb), pl.ds(pl.multiple_of((p + off) * tn, LANES), tn)])
        in_specs.append(pl.BlockSpec(memory_space=pl.ANY))
        args.append(a)
    for a, kind, off in extras:
        if kind == "tile":
            in_specs.append(pl.BlockSpec((tm, tn), lambda n, i, off=off: (i, n + off)))
        else:
            in_specs.append(pl.BlockSpec((1, tn), lambda n, i, off=off: (0, n + off)))
        args.append(a)
    panel_shapes = [(tn, kb) if wt else (kb, tn) for _, kb, _, _ in ws]
    cast_rows = LANES
    for shp in panel_shapes:
        assert shp[0] % cast_rows == 0, shp
    scratch = ([pltpu.VMEM(shp, F32) for shp in panel_shapes] + [pltpu.VMEM(shp, BF16) for shp in panel_shapes]
               + [pltpu.SemaphoreType.DMA((len(ws), FIRST_PANEL_CHUNKS))])
    kern = functools.partial(_ws_kernel, nx=len(xs), nw=len(ws), ne=len(extras), pairs=pairs, epilogue=epilogue,
                             cast_rows=cast_rows, wt=wt, panel_src=panel_src)
    return pl.pallas_call(
        kern,
        grid=(n_out // tn, m // tm),
        in_specs=in_specs,
        out_specs=pl.BlockSpec((tm, tn), lambda n, i: (i, n)),
        out_shape=jax.ShapeDtypeStruct((m, n_out), out_dtype),
        scratch_shapes=scratch,
        compiler_params=_params("arbitrary", "arbitrary"),
        name=name,
    )(*args)


def _ep_plain(dots, extras):
    return dots[0]


def _sigmoid(x):
    return 0.5 * jnp.tanh(0.5 * x) + 0.5


def _ep_bias_sigmoid(dots, extras):
    return _sigmoid(dots[0] + extras[0])


def _ep_residual(dots, extras):
    return extras[0] + dots[0]


def _ep_swiglu(dots, extras):
    g, u = dots
    return g * _sigmoid(g) * u


def _ep_merge(dots, extras):
    return (extras[0].astype(F32) * dots[0] + extras[1].astype(F32) * dots[1]
            + extras[2].astype(F32) * dots[2])


def _log_sigmoid(x):
    return jnp.minimum(x, 0.0) - jnp.log1p(jnp.exp(-jnp.abs(x)))


def _cumsum_lanes(x, seg):
    n = x.shape[-1]
    pos = lax.broadcasted_iota(jnp.int32, x.shape, 1) % seg
    shift = 1
    while shift < seg:
        x = x + jnp.where(pos >= shift, pltpu.roll(x, shift, axis=1), 0.0)
        shift *= 2
    del n
    return x


def _gates_kernel(g_ref, bias_ref, rows_ref, *, chunk):
    g = g_ref[...] + bias_ref[...]
    gt = g.T
    s = gt.shape[1]
    rows_ref[0:8, :] = _cumsum_lanes(_log_sigmoid(gt[64:72, :]), s) * LOG2E
    rows_ref[8:16, :] = gt[72:80, :]
    rows_ref[16:24, :] = _cumsum_lanes(_log_sigmoid(gt[80:88, :]), chunk)


def _gate_rows(g_pre, bias_vec, b, s):
    return pl.pallas_call(
        functools.partial(_gates_kernel, chunk=ML_CHUNK),
        grid=(b,),
        in_specs=[pl.BlockSpec((s, LANES), lambda i: (i, 0)),
                  pl.BlockSpec((1, LANES), lambda i: (0, 0))],
        out_specs=pl.BlockSpec((None, 24, s), lambda i: (i, 0, 0)),
        out_shape=jax.ShapeDtypeStruct((b, 24, s), F32),
        compiler_params=_params("parallel"),
        name="gate_rows",
    )(g_pre, bias_vec)


def _flash_kernel(*refs, blk, has_bias):
    if has_bias:
        q_ref, kraw_ref, v_ref, d_ref, qg_ref, kg_ref, o_ref, k_ref = refs
    else:
        q_ref, k_ref, v_ref, o_ref = refs
        d_ref = None
    h = pl.program_id(1)
    s_len = k_ref.shape[0]
    row = lax.broadcasted_iota(jnp.int32, (blk, blk), 0)
    col = lax.broadcasted_iota(jnp.int32, (blk, blk), 1)
    nt = (((1,), (1,)), ((), ()))

    def qk_norm(x, gain):
        xf = x.astype(F32)
        return xf * lax.rsqrt(jnp.mean(xf * xf, axis=-1, keepdims=True) + NORM_EPS) * gain

    if has_bias:
        for c in range(s_len // blk):
            k_ref[c * blk:(c + 1) * blk, :] = qk_norm(kraw_ref[c * blk:(c + 1) * blk, :], kg_ref[...]).astype(BF16)
    for qi in range(s_len // blk):
        lo = qi * blk
        q = q_ref[lo:lo + blk, :]
        if has_bias:
            q = (qk_norm(q, qg_ref[...]) * (FOX_DIM ** -0.5 * LOG2E)).astype(BF16)
        s_diag = lax.dot_general(q, k_ref[lo:lo + blk, :], nt, preferred_element_type=F32)
        if has_bias:
            s_diag = s_diag - d_ref[pl.ds(h, 1), lo:lo + blk]
        s_diag = jnp.where(col <= row, s_diag, -jnp.inf)
        m = jnp.max(s_diag, axis=-1, keepdims=True)
        if qi:
            s_off = lax.dot_general(q, k_ref[0:lo, :], nt, preferred_element_type=F32)
            if has_bias:
                s_off = s_off - d_ref[pl.ds(h, 1), 0:lo]
            m = jnp.maximum(m, jnp.max(s_off, axis=-1, keepdims=True))
        p_diag = jnp.exp2(s_diag - m)
        l = jnp.sum(p_diag, axis=-1, keepdims=True)
        acc = jnp.dot(p_diag.astype(BF16), v_ref[lo:lo + blk, :], preferred_element_type=F32)
        if qi:
            p_off = jnp.exp2(s_off - m)
            l = l + jnp.sum(p_off, axis=-1, keepdims=True)
            acc = acc + jnp.dot(p_off.astype(BF16), v_ref[0:lo, :], preferred_element_type=F32)
        o_ref[lo:lo + blk, :] = (acc / l).astype(o_ref.dtype)


def _flash(q_arr, q_off, k_arr, k_off, v_arr, v_off, dk, n_heads, out_width, fox=None):
    b, s, _ = q_arr.shape
    blk = ATT_BLOCK
    in_specs = [
        pl.BlockSpec((None, s, dk), lambda bi, h: (bi, 0, q_off + h)),
        pl.BlockSpec((None, s, dk), lambda bi, h: (bi, 0, k_off + h)),
        pl.BlockSpec((None, s, LANES), lambda bi, h: (bi, 0, v_off + h)),
    ]
    args = [q_arr, k_arr, v_arr]
    scratch = []
    if fox is not None:
        rows, q_gain, k_gain = fox
        in_specs += [pl.BlockSpec((None, 8, s), lambda bi, h: (bi, 0, 0)),
                     pl.BlockSpec((1, dk), lambda bi, h: (0, 0)),
                     pl.BlockSpec((1, dk), lambda bi, h: (0, 0))]
        args += [rows, q_gain.reshape(1, dk), k_gain.reshape(1, dk)]
        scratch = [pltpu.VMEM((s, dk), BF16)]
    return pl.pallas_call(
        functools.partial(_flash_kernel, blk=blk, has_bias=fox is not None),
        grid=(b, n_heads),
        in_specs=in_specs,
        out_specs=pl.BlockSpec((None, s, LANES), lambda bi, h: (bi, 0, h)),
        out_shape=jax.ShapeDtypeStruct((b, s, out_width), BF16),
        scratch_shapes=scratch,
        compiler_params=_params("parallel", "parallel"),
        name="flash_bias" if fox is not None else "flash",
    )(*args)


def _mlstm_kernel(q_ref, k_ref, v_ref, og_ref, rows_ref, gain_ref, wq_ref, wk_ref, bq_ref, bk_ref, o_ref,
                  xq_ref, xk_ref, *, chunk):
    L = chunk
    hp = pl.program_id(1)
    s = q_ref.shape[0]
    nc = s // L
    for x_ref, xs_ref in ((q_ref, xq_ref), (k_ref, xk_ref)):
        xs_ref[0:PAD, :] = jnp.zeros((PAD, LANES), F32)
        for c in range(nc):
            xs_ref[PAD + c * L:PAD + (c + 1) * L, :] = x_ref[c * L:(c + 1) * L, :].astype(F32)

    def conv_silu(xs_ref, w_ref, b_ref, t0):
        y = b_ref[...]
        for j in range(ML_CONV):
            off = PAD + t0 - (ML_CONV - 1) + j
            y = y + w_ref[j:j + 1, :] * xs_ref[off:off + L, :]
        return y * jax.nn.sigmoid(y)

    lane = lax.broadcasted_iota(jnp.int32, (1, LANES), 1)
    row = lax.broadcasted_iota(jnp.int32, (L, L), 0)
    col = lax.broadcasted_iota(jnp.int32, (L, L), 1)
    causal = col <= row
    eye = col == row
    col1 = lax.broadcasted_iota(jnp.int32, (1, L), 1)
    row1 = lax.broadcasted_iota(jnp.int32, (L, 1), 0)

    def to_col(r):
        return jnp.sum(jnp.where(eye, r, 0.0), axis=-1, keepdims=True)

    def head_step(t0, hh, qp, kp, kp_b, state):
        c_state, n_state, m_prev = state
        head = 2 * hp + hh
        li = rows_ref[pl.ds(8 + head, 1), pl.ds(t0, L)]
        bc = rows_ref[pl.ds(16 + head, 1), pl.ds(t0, L)]
        a_row = li - bc
        bc_col = to_col(bc)
        log_w = jnp.where(causal, bc_col + a_row, -jnp.inf)
        log_inter = bc_col + m_prev
        m_t = jnp.maximum(log_inter, jnp.max(log_w, axis=-1, keepdims=True))
        w = jnp.exp(log_w - m_t)
        inter = jnp.exp(log_inter - m_t)
        head_lanes = (lane >= hh * ML_QK) & (lane < (hh + 1) * ML_QK)
        qh = jnp.where(head_lanes, qp, jnp.zeros_like(qp))
        qk = lax.dot_general(qh, kp_b, (((1,), (1,)), ((), ())), preferred_element_type=F32) * w
        vh = v_ref[pl.ds(t0, L), hh * ML_V:(hh + 1) * ML_V]
        num = (jnp.dot(qk.astype(BF16), vh, preferred_element_type=F32)
               + inter * jnp.dot(qh, c_state.astype(BF16), preferred_element_type=F32))
        den = (jnp.sum(qk, axis=-1, keepdims=True)
               + inter * jnp.sum(qh.astype(F32) * n_state, axis=-1, keepdims=True))
        h_out = num / jnp.maximum(jnp.abs(den), jnp.exp(-m_t))
        hn = h_out * lax.rsqrt(jnp.mean(h_out * h_out, axis=-1, keepdims=True) + NORM_EPS)
        hn = hn * gain_ref[:, hh * ML_V:(hh + 1) * ML_V]
        og = og_ref[pl.ds(t0, L), hh * ML_V:(hh + 1) * ML_V].astype(F32)
        o_ref[pl.ds(t0, L), hh * ML_V:(hh + 1) * ML_V] = (hn * jax.nn.sigmoid(og)).astype(o_ref.dtype)
        m_new = jnp.sum(jnp.where(row1 == L - 1, m_t, 0.0), axis=0, keepdims=True)
        b_last = jnp.sum(jnp.where(col1 == L - 1, bc, 0.0), axis=-1, keepdims=True)
        decay = jnp.exp(b_last + m_prev - m_new)
        w_last = to_col(jnp.exp(b_last + a_row - m_new))
        kw = kp * w_last
        c_upd = lax.dot_general(kw.astype(BF16), vh, (((0,), (0,)), ((), ())), preferred_element_type=F32)
        return decay * c_state + c_upd, decay * n_state + jnp.sum(kw, axis=0, keepdims=True), m_new

    states = [(jnp.zeros((LANES, ML_V), F32), jnp.zeros((1, LANES), F32), jnp.zeros((1, 1), F32))
              for _ in range(2)]
    for ci in range(nc):
        t0 = ci * L
        qp = conv_silu(xq_ref, wq_ref, bq_ref, t0).astype(BF16)
        kp = conv_silu(xk_ref, wk_ref, bk_ref, t0) * (ML_QK ** -0.5)
        kp_b = kp.astype(BF16)
        states = [head_step(t0, hh, qp, kp, kp_b, states[hh]) for hh in range(2)]


def _mlstm(qkv_arr, og_arr, rows, out_gain, conv_w, conv_b):
    b, s, _ = qkv_arr.shape
    npair = ML_HEADS // 2
    v_blk0 = 2 * ML_HEADS * ML_QK // (2 * ML_V)
    pair = lambda off: (lambda bi, hp: (0, off + hp))
    return pl.pallas_call(
        functools.partial(_mlstm_kernel, chunk=ML_CHUNK),
        grid=(b, npair),
        in_specs=[pl.BlockSpec((None, s, LANES), lambda bi, hp: (bi, 0, hp)),
                  pl.BlockSpec((None, s, LANES), lambda bi, hp: (bi, 0, npair + hp)),
                  pl.BlockSpec((None, s, 2 * ML_V), lambda bi, hp: (bi, 0, v_blk0 + hp)),
                  pl.BlockSpec((None, s, 2 * ML_V), lambda bi, hp: (bi, 0, hp)),
                  pl.BlockSpec((None, 24, s), lambda bi, hp: (bi, 0, 0)),
                  pl.BlockSpec((1, 2 * ML_V), pair(0)),
                  pl.BlockSpec((ML_CONV, LANES), pair(0)),
                  pl.BlockSpec((ML_CONV, LANES), pair(npair)),
                  pl.BlockSpec((1, LANES), pair(0)),
                  pl.BlockSpec((1, LANES), pair(npair))],
        out_specs=pl.BlockSpec((None, s, 2 * ML_V), lambda bi, hp: (bi, 0, hp)),
        out_shape=jax.ShapeDtypeStruct((b, s, ML_HEADS * ML_V), BF16),
        scratch_shapes=[pltpu.VMEM((s + PAD, LANES), F32), pltpu.VMEM((s + PAD, LANES), F32)],
        compiler_params=_params("parallel", "parallel"),
        name="mlstm",
    )(qkv_arr, qkv_arr, qkv_arr, og_arr, rows, out_gain.reshape(1, -1), conv_w, conv_w,
      conv_b.reshape(1, -1), conv_b.reshape(1, -1))


def _mla_prep_kernel(s_ref, cos_ref, sin_ref, wuq_ref, wukv_ref, cqg_ref, ckvg_ref, qgn_ref, qgr_ref,
                     kgn_ref, kgr_ref, q_ref, k_ref, v_ref):
    scale = MLA_QK ** -0.5 * LOG2E
    lane = lax.broadcasted_iota(jnp.int32, (1, LANES), 1)
    cosf = cos_ref[...]
    sinf = sin_ref[...]

    def norm(x, g):
        return x * lax.rsqrt(jnp.mean(x * x, axis=-1, keepdims=True) + NORM_EPS) * g

    def rope(x):
        swap = jnp.where(lane < MLA_ROPE // 2, pltpu.roll(x, LANES - MLA_ROPE // 2, axis=1),
                         pltpu.roll(x, MLA_ROPE // 2, axis=1))
        return x * cosf + swap * sinf

    cq = s_ref[:, 0:MLA_Q_RANK]
    ckv = s_ref[:, MLA_Q_RANK:MLA_Q_RANK + MLA_KV_RANK]
    kr = jnp.where(lane < MLA_ROPE, s_ref[:, MLA_Q_RANK + MLA_KV_RANK:], 0.0)
    qf = jnp.dot(norm(cq, cqg_ref[...]).astype(BF16), wuq_ref[...], preferred_element_type=F32)
    kvf = jnp.dot(norm(ckv, ckvg_ref[...]).astype(BF16), wukv_ref[...], preferred_element_type=F32)
    ss_kr = jnp.sum(kr * kr, axis=-1, keepdims=True)
    kr_rot = rope(kr * kgr_ref[...])
    qgn_s = qgn_ref[...] * scale
    qgr_s = qgr_ref[...] * scale
    nw = MLA_HEADS * MLA_NOPE
    for h in range(MLA_HEADS):
        qn = qf[:, h * MLA_NOPE:(h + 1) * MLA_NOPE]
        qr = qf[:, nw + h * LANES:nw + (h + 1) * LANES]
        ss = jnp.sum(qn * qn, axis=-1, keepdims=True) + jnp.sum(qr * qr, axis=-1, keepdims=True)
        r = lax.rsqrt(ss / MLA_QK + NORM_EPS)
        q_ref[:, h * MLA_PAD:h * MLA_PAD + MLA_NOPE] = ((qn * r) * qgn_s).astype(q_ref.dtype)
        q_ref[:, h * MLA_PAD + MLA_NOPE:(h + 1) * MLA_PAD] = rope((qr * r) * qgr_s).astype(q_ref.dtype)
        kn = kvf[:, h * 2 * MLA_NOPE:h * 2 * MLA_NOPE + MLA_NOPE]
        rk = lax.rsqrt((jnp.sum(kn * kn, axis=-1, keepdims=True) + ss_kr) / MLA_QK + NORM_EPS)
        k_ref[:, h * MLA_PAD:h * MLA_PAD + MLA_NOPE] = ((kn * rk) * kgn_ref[...]).astype(k_ref.dtype)
        k_ref[:, h * MLA_PAD + MLA_NOPE:(h + 1) * MLA_PAD] = (kr_rot * rk).astype(k_ref.dtype)
        v_ref[:, h * MLA_V:(h + 1) * MLA_V] = kvf[:, h * 2 * MLA_NOPE + MLA_NOPE:(h + 1) * 2 * MLA_NOPE].astype(v_ref.dtype)


def _mla_prep(seg3, cosf, sinf, wuq_p, wukv_b, cq_gain, ckv_gain, q_gain, k_gain, tm=256):
    m, w3 = seg3.shape
    pad = jnp.zeros((LANES - MLA_ROPE,), F32)
    qgn = q_gain[:MLA_NOPE].reshape(1, -1)
    qgr = jnp.concatenate([q_gain[MLA_NOPE:], pad]).reshape(1, -1)
    kgn = k_gain[:MLA_NOPE].reshape(1, -1)
    kgr = jnp.concatenate([k_gain[MLA_NOPE:], pad]).reshape(1, -1)
    const = lambda i: (0, 0)
    return pl.pallas_call(
        _mla_prep_kernel,
        grid=(m // tm,),
        in_specs=[pl.BlockSpec((tm, w3), lambda i: (i, 0)),
                  pl.BlockSpec((tm, LANES), lambda i: (i, 0)),
                  pl.BlockSpec((tm, LANES), lambda i: (i, 0)),
                  pl.BlockSpec(wuq_p.shape, const),
                  pl.BlockSpec(wukv_b.shape, const),
                  pl.BlockSpec((1, MLA_Q_RANK), const),
                  pl.BlockSpec((1, MLA_KV_RANK), const),
                  pl.BlockSpec((1, LANES), const),
                  pl.BlockSpec((1, LANES), const),
                  pl.BlockSpec((1, LANES), const),
                  pl.BlockSpec((1, LANES), const)],
        out_specs=[pl.BlockSpec((tm, MLA_HEADS * MLA_PAD), lambda i: (i, 0)),
                   pl.BlockSpec((tm, MLA_HEADS * MLA_PAD), lambda i: (i, 0)),
                   pl.BlockSpec((tm, MLA_HEADS * MLA_V), lambda i: (i, 0))],
        out_shape=[jax.ShapeDtypeStruct((m, MLA_HEADS * MLA_PAD), BF16),
                   jax.ShapeDtypeStruct((m, MLA_HEADS * MLA_PAD), BF16),
                   jax.ShapeDtypeStruct((m, MLA_HEADS * MLA_V), BF16)],
        compiler_params=_params("parallel"),
        name="mla_prep",
    )(seg3, cosf, sinf, wuq_p, wukv_b, cq_gain.reshape(1, -1), ckv_gain.reshape(1, -1), qgn, qgr, kgn, kgr)


_FW = FOX_HEADS * FOX_DIM
_IN_SIZES = (_FW, _FW, _FW, FOX_HEADS, ML_HEADS * ML_QK, ML_HEADS * ML_QK, ML_HEADS * ML_V,
             ML_HEADS, ML_HEADS, ML_HEADS * ML_V, MLA_Q_RANK, MLA_KV_RANK, MLA_ROPE)
_IN_OFFS = [sum(_IN_SIZES[:i]) for i in range(len(_IN_SIZES) + 1)]
(_C_FQ, _, _, _C_FF, _C_MQ, _, _, _C_MI, _C_MF, _C_MO, _C_CQ, _, _, _C_GATES) = _IN_OFFS


def _small_gate_weight(w_in_t, l):
    d = w_in_t.shape[2]
    rows = [jnp.zeros((MLA_ROPE, d), w_in_t.dtype),
            w_in_t[l, _C_FF:_C_FF + FOX_HEADS],
            w_in_t[l, _C_MI:_C_MI + ML_HEADS],
            w_in_t[l, _C_MF:_C_MF + ML_HEADS],
            jnp.zeros((LANES - MLA_ROPE - FOX_HEADS - 2 * ML_HEADS, d), w_in_t.dtype)]
    return jnp.concatenate(rows, axis=0)


def _layer(x, cosf, sinf, b, s, p, big, l):
    m, d = x.shape
    h1, g_pre = _rmsnorm(x, p["mix_norm"], _small_gate_weight(big["w_in_t"], l))
    w_in = (big["w_in_t"], l)

    def in_proj(first_col, width, out_dtype, name, extras=(), epilogue=_ep_plain, tn=1024):
        return _ws_matmul([(h1, d, 0)], [(w_in, d, 0, 0)], list(extras), epilogue, tm=1024, tn=tn, n_out=width,
                          out_dtype=out_dtype, first_row=first_col, name=name)

    proj_fox = in_proj(_C_FQ, 3 * _FW, BF16, "proj_fox")
    proj_ml = in_proj(_C_MQ, 2 * ML_HEADS * ML_QK + ML_HEADS * ML_V, BF16, "proj_ml")
    proj_mo = in_proj(_C_MO, ML_HEADS * ML_V, BF16, "proj_mo")
    w_mla = MLA_Q_RANK + MLA_KV_RANK + LANES
    seg3 = in_proj(_C_CQ, w_mla, F32, "proj_mla", tn=w_mla // 2)
    gates = in_proj(_C_GATES, 3 * d, BF16, "gates", [(p["gate_bias"].reshape(1, -1), "row", 0)], _ep_bias_sigmoid)

    bias_vec = jnp.concatenate([jnp.zeros((MLA_ROPE,), F32), p["fox_f_bias"], p["mlstm_i_bias"], p["mlstm_f_bias"],
                                jnp.zeros((LANES - MLA_ROPE - 24,), F32)]).reshape(1, LANES)
    rows = _gate_rows(g_pre, bias_vec, b, s)

    proj_fox_3d = proj_fox.reshape(b, s, -1)
    o_fox = _flash(proj_fox_3d, 0, proj_fox_3d, FOX_HEADS, proj_fox_3d, 2 * FOX_HEADS, FOX_DIM, FOX_HEADS,
                   FOX_HEADS * FOX_DIM, fox=(rows, p["fox_q_gain"], p["fox_k_gain"])).reshape(m, -1)
    o_ml = _mlstm(proj_ml.reshape(b, s, -1), proj_mo.reshape(b, s, -1), rows, p["mlstm_out_gain"],
                  p["mlstm_conv_w"], p["mlstm_conv_b"]).reshape(m, -1)
    wuq = p["mla_w_uq"].reshape(MLA_Q_RANK, MLA_HEADS, MLA_QK)
    wuq_p = jnp.concatenate([
        wuq[:, :, :MLA_NOPE].reshape(MLA_Q_RANK, -1),
        jnp.pad(wuq[:, :, MLA_NOPE:], ((0, 0), (0, 0), (0, LANES - MLA_ROPE))).reshape(MLA_Q_RANK, -1)],
        axis=1).astype(BF16)
    q_mla, k_mla, v_mla = _mla_prep(seg3, cosf, sinf, wuq_p, p["mla_w_ukv"].astype(BF16), p["mla_cq_gain"],
                                    p["mla_ckv_gain"], p["mla_q_gain"], p["mla_k_gain"])
    o_mla = _flash(q_mla.reshape(b, s, -1), 0, k_mla.reshape(b, s, -1), 0, v_mla.reshape(b, s, -1), 0,
                   MLA_PAD, MLA_HEADS, MLA_HEADS * MLA_V).reshape(m, -1)

    kb = o_fox.shape[1]
    tn_merge = 1024
    nb = d // tn_merge
    merged = _ws_matmul([(o_fox, kb, 0), (o_ml, kb, 0), (o_mla, kb, 0)],
                        [((big["w_fox_out"], l), kb, 0, 0), ((big["w_mlstm_out"], l), kb, 0, 0),
                         ((big["w_mla_out"], l), kb, 0, 0)],
                        [(gates, "tile", 0), (gates, "tile", nb), (gates, "tile", 2 * nb)],
                        _ep_merge, tm=1024, tn=tn_merge, n_out=d, out_dtype=BF16, name="merge")
    x = _ws_matmul([(merged, d, 0)], [((big["w_o"], l), d, 0, 0)], [(x, "tile", 0)], _ep_residual,
                   tm=1024, tn=512, n_out=d, out_dtype=F32, name="w_o")

    h2 = _rmsnorm(x, p["ffn_norm"])
    d_ff = big["w_gate"].shape[2]
    act = _ws_matmul([(h2, d, 0)], [((big["w_gate"], l), d, 0, 0), ((big["w_up"], l), d, 0, 0)], [], _ep_swiglu,
                     tm=2048, tn=256, n_out=d_ff, out_dtype=BF16, pairs=[(0, 0), (0, 1)], name="gate_up")
    kh = d_ff // 2
    for ki in range(2):
        x = _ws_matmul([(act, kh, ki)], [((big["w_down"], l), kh, ki, 0)], [(x, "tile", 0)], _ep_residual,
                       tm=1024, tn=512, n_out=d, out_dtype=F32, name=f"down{ki}")
    return x


def kernel(x, positions, mix_norm, w_in, fox_f_bias, fox_q_gain, fox_k_gain, mlstm_conv_w, mlstm_conv_b,
           mlstm_i_bias, mlstm_f_bias, mlstm_out_gain, mla_cq_gain, mla_ckv_gain, mla_w_uq, mla_w_ukv,
           mla_q_gain, mla_k_gain, w_fox_out, w_mlstm_out, w_mla_out, gate_bias, w_o, ffn_norm, w_gate,
           w_up, w_down):
    b, s, d = x.shape
    depth = w_in.shape[0]
    inv_freq = jnp.power(ROPE_THETA, -jnp.arange(0, MLA_ROPE, 2, dtype=F32) / MLA_ROPE)
    ang = positions.astype(F32).reshape(b * s, 1) * inv_freq
    zpad = jnp.zeros((b * s, LANES - MLA_ROPE), F32)
    cosf = jnp.concatenate([jnp.cos(ang), jnp.cos(ang), zpad], axis=1)
    sinf = jnp.concatenate([-jnp.sin(ang), jnp.sin(ang), zpad], axis=1)
    names = dict(mix_norm=mix_norm, fox_f_bias=fox_f_bias, fox_q_gain=fox_q_gain, fox_k_gain=fox_k_gain,
                 mlstm_conv_w=mlstm_conv_w, mlstm_conv_b=mlstm_conv_b, mlstm_i_bias=mlstm_i_bias,
                 mlstm_f_bias=mlstm_f_bias, mlstm_out_gain=mlstm_out_gain, mla_cq_gain=mla_cq_gain,
                 mla_ckv_gain=mla_ckv_gain, mla_w_uq=mla_w_uq, mla_w_ukv=mla_w_ukv, mla_q_gain=mla_q_gain,
                 mla_k_gain=mla_k_gain, gate_bias=gate_bias, ffn_norm=ffn_norm)
    big = dict(w_in_t=jnp.swapaxes(w_in, 1, 2), w_fox_out=w_fox_out, w_mlstm_out=w_mlstm_out, w_mla_out=w_mla_out, w_o=w_o, w_gate=w_gate,
               w_up=w_up, w_down=w_down)
    xf = x.reshape(b * s, d)
    for l in range(depth):
        xf = _layer(xf, cosf, sinf, b, s, {k: v[l] for k, v in names.items()}, big, l)
    return xf.reshape(b, s, d)
```

```python
import functools

import jax
import jax.numpy as jnp
from jax import lax
from jax.experimental import pallas as pl
from jax.experimental.pallas import tpu as pltpu

F32 = jnp.float32
BF16 = jnp.bfloat16

NORM_EPS = 1e-6
ROPE_THETA = 10000.0
LOG2E = 1.4426950408889634

FOX_HEADS = 8
FOX_DIM = 128
ML_HEADS = 8
ML_QK = 64
ML_V = 128
ML_CONV = 4
MLA_HEADS = 8
MLA_Q_RANK = 896
MLA_KV_RANK = 512
MLA_NOPE = 128
MLA_ROPE = 64
MLA_V = 128
MLA_QK = MLA_NOPE + MLA_ROPE
MLA_PAD = 256

LANES = 128
SUBLANES = 8
MXU_DEPTH = 256
VMEM_LIMIT = 56 * 1024 * 1024
ML_CHUNK = 256
ATT_BLOCK = 256
PAD = 8


def _params(*sem):
    return pltpu.CompilerParams(dimension_semantics=sem, vmem_limit_bytes=VMEM_LIMIT)


def _rmsnorm_kernel(x_ref, g_ref, *rest):
    x = x_ref[...]
    ms = jnp.mean(x * x, axis=-1, keepdims=True)
    h = (x * lax.rsqrt(ms + NORM_EPS) * g_ref[...]).astype(BF16)
    if len(rest) == 1:
        rest[0][...] = h
    else:
        w_ref, o_ref, gp_ref = rest
        o_ref[...] = h
        gp_ref[...] = lax.dot_general(h, w_ref[...].astype(BF16), (((1,), (1,)), ((), ())),
                                      preferred_element_type=F32)


def _rmsnorm(x, gain, w_small_t=None, tm=256):
    m, d = x.shape
    in_specs = [pl.BlockSpec((tm, d), lambda i: (i, 0)),
                pl.BlockSpec((1, d), lambda i: (0, 0))]
    out_specs = pl.BlockSpec((tm, d), lambda i: (i, 0))
    out_shape = jax.ShapeDtypeStruct((m, d), BF16)
    args = [x, gain.reshape(1, d)]
    if w_small_t is not None:
        in_specs.append(pl.BlockSpec(w_small_t.shape, lambda i: (0, 0)))
        out_specs = [out_specs, pl.BlockSpec((tm, LANES), lambda i: (i, 0))]
        out_shape = [out_shape, jax.ShapeDtypeStruct((m, LANES), F32)]
        args.append(w_small_t)
    return pl.pallas_call(
        _rmsnorm_kernel,
        grid=(m // tm,),
        in_specs=in_specs,
        out_specs=out_specs,
        out_shape=out_shape,
        compiler_params=_params("parallel"),
        name="rmsnorm",
    )(*args)


def _ws_kernel(*refs, nx, nw, ne, pairs, epilogue, cast_rows, wt, panel_src):
    x_refs = refs[:nx]
    w_hbm = refs[nx:nx + nw]
    e_refs = refs[nx + nw:nx + nw + ne]
    o_ref = refs[nx + nw + ne]
    scratch = refs[nx + nw + ne + 1:]
    stage, w_bf, sem = scratch[:nw], scratch[nw:2 * nw], scratch[2 * nw]
    n = pl.program_id(0)
    n_panels = pl.num_programs(0)

    def panel_copy(wi, panel):
        return pltpu.make_async_copy(panel_src[wi](w_hbm[wi], panel), stage[wi], sem.at[wi])

    @pl.when(pl.program_id(1) == 0)
    def _():
        @pl.when(n == 0)
        def _():
            for wi in range(nw):
                panel_copy(wi, 0).start()

        for wi in range(nw):
            panel_copy(wi, n).wait()

            def body(i, c, wi=wi):
                r = pl.multiple_of(i * cast_rows, cast_rows)
                w_bf[wi][pl.ds(r, cast_rows), :] = stage[wi][pl.ds(r, cast_rows), :].astype(BF16)
                return c

            lax.fori_loop(0, stage[wi].shape[0] // cast_rows, body, 0)

        @pl.when(n + 1 < n_panels)
        def _():
            for wi in range(nw):
                panel_copy(wi, n + 1).start()

    dims = (((1,), (1 if wt else 0,)), ((), ()))
    dots = [lax.dot_general(x_refs[xi][...], w_bf[wi][...], dims, preferred_element_type=F32) for xi, wi in pairs]
    o_ref[...] = epilogue(dots, [e[...] for e in e_refs]).astype(o_ref.dtype)


def _ws_matmul(xs, ws, extras, epilogue, *, tm, tn, n_out, out_dtype, pairs=None, wt=False, first_row=None, name):
    m = xs[0][0].shape[0]
    assert n_out % tn == 0 and m % tm == 0
    pairs = pairs or [(i, i) for i in range(len(ws))]
    wt = wt or first_row is not None
    in_specs = []
    args = []
    panel_src = []
    for a, kb, k0 in xs:
        if k0 % kb == 0:
            in_specs.append(pl.BlockSpec((tm, kb), lambda n, i, kblk=k0 // kb: (i, kblk)))
        else:
            assert k0 % LANES == 0 and k0 + kb <= a.shape[1]
            in_specs.append(pl.BlockSpec((pl.Element(tm), pl.Element(kb)), lambda n, i, k0=k0: (i * tm, k0)))
        args.append(a)
    for a, kb, k0, off in ws:
        layer = None
        if isinstance(a, tuple):
            a, layer = a
        assert a.dtype == F32
        if first_row is not None:
            assert len(ws) == 1 and kb == a.shape[2] and first_row % SUBLANES == 0 and tn % SUBLANES == 0
            assert first_row + n_out <= a.shape[1]
            panel_src.append(lambda w, p, layer=layer: w.at[
                layer, pl.ds(pl.multiple_of(first_row + p * tn, SUBLANES), tn), :])
        elif wt:
            assert layer is None and kb == a.shape[1]
            panel_src.append(lambda w, p: w.at[pl.ds(pl.multiple_of(p * tn, SUBLANES), tn), :])
        else:
            assert layer is not None
            panel_src.append(lambda w, p, layer=layer, kb=kb, k0=k0, off=off: w.at[
                layer, pl.ds(k0, kb), pl.ds(pl.multiple_of((p + off) * tn, LANES), tn)])
        in_specs.append(pl.BlockSpec(memory_space=pl.ANY))
        args.append(a)
    for a, kind, off in extras:
        if kind == "tile":
            in_specs.append(pl.BlockSpec((tm, tn), lambda n, i, off=off: (i, n + off)))
        else:
            in_specs.append(pl.BlockSpec((1, tn), lambda n, i, off=off: (0, n + off)))
        args.append(a)
    panel_shapes = [(tn, kb) if wt else (kb, tn) for _, kb, _, _ in ws]
    cast_rows = LANES
    for shp in panel_shapes:
        assert shp[0] % cast_rows == 0, shp
    scratch = ([pltpu.VMEM(shp, F32) for shp in panel_shapes] + [pltpu.VMEM(shp, BF16) for shp in panel_shapes]
               + [pltpu.SemaphoreType.DMA((len(ws),))])
    kern = functools.partial(_ws_kernel, nx=len(xs), nw=len(ws), ne=len(extras), pairs=pairs, epilogue=epilogue,
                             cast_rows=cast_rows, wt=wt, panel_src=panel_src)
    return pl.pallas_call(
        kern,
        grid=(n_out // tn, m // tm),
        in_specs=in_specs,
        out_specs=pl.BlockSpec((tm, tn), lambda n, i: (i, n)),
        out_shape=jax.ShapeDtypeStruct((m, n_out), out_dtype),
        scratch_shapes=scratch,
        compiler_params=_params("arbitrary", "arbitrary"),
        name=name,
    )(*args)


def _ep_plain(dots, extras):
    return dots[0]


def _sigmoid(x):
    return 0.5 * jnp.tanh(0.5 * x) + 0.5


def _ep_bias_sigmoid(dots, extras):
    return _sigmoid(dots[0] + extras[0])


def _ep_residual(dots, extras):
    return extras[0] + dots[0]


def _ep_swiglu(dots, extras):
    g, u = dots
    return g * _sigmoid(g) * u


def _ep_merge(dots, extras):
    return (extras[0].astype(F32) * dots[0] + extras[1].astype(F32) * dots[1]
            + extras[2].astype(F32) * dots[2])


def _log_sigmoid(x):
    return jnp.minimum(x, 0.0) - jnp.log1p(jnp.exp(-jnp.abs(x)))


def _cumsum_lanes(x, seg):
    n = x.shape[-1]
    pos = lax.broadcasted_iota(jnp.int32, x.shape, 1) % seg
    shift = 1
    while shift < seg:
        x = x + jnp.where(pos >= shift, pltpu.roll(x, shift, axis=1), 0.0)
        shift *= 2
    del n
    return x


def _gates_kernel(g_ref, bias_ref, rows_ref, *, chunk):
    g = g_ref[...] + bias_ref[...]
    gt = g.T
    s = gt.shape[1]
    rows_ref[0:8, :] = _cumsum_lanes(_log_sigmoid(gt[64:72, :]), s) * LOG2E
    rows_ref[8:16, :] = gt[72:80, :] * LOG2E
    rows_ref[16:24, :] = _cumsum_lanes(_log_sigmoid(gt[80:88, :]), chunk) * LOG2E


def _gate_rows(g_pre, bias_vec, b, s):
    return pl.pallas_call(
        functools.partial(_gates_kernel, chunk=ML_CHUNK),
        grid=(b,),
        in_specs=[pl.BlockSpec((s, LANES), lambda i: (i, 0)),
                  pl.BlockSpec((1, LANES), lambda i: (0, 0))],
        out_specs=pl.BlockSpec((None, 24, s), lambda i: (i, 0, 0)),
        out_shape=jax.ShapeDtypeStruct((b, 24, s), F32),
        compiler_params=_params("parallel"),
        name="gate_rows",
    )(g_pre, bias_vec)


def _flash_kernel(*refs, blk, has_bias):
    if has_bias:
        q_ref, kraw_ref, v_ref, d_ref, qg_ref, kg_ref, o_ref, k_ref = refs
    else:
        q_ref, k_ref, v_ref, o_ref = refs
        d_ref = None
    h = pl.program_id(1)
    s_len = k_ref.shape[0]
    row = lax.broadcasted_iota(jnp.int32, (blk, blk), 0)
    col = lax.broadcasted_iota(jnp.int32, (blk, blk), 1)
    nt = (((1,), (1,)), ((), ()))

    def qk_norm(x, gain):
        xf = x.astype(F32)
        return xf * lax.rsqrt(jnp.mean(xf * xf, axis=-1, keepdims=True) + NORM_EPS) * gain

    if has_bias:
        for c in range(s_len // blk):
            k_ref[c * blk:(c + 1) * blk, :] = qk_norm(kraw_ref[c * blk:(c + 1) * blk, :], kg_ref[...]).astype(BF16)
    for qi in range(s_len // blk):
        lo = qi * blk
        q = q_ref[lo:lo + blk, :]
        if has_bias:
            q = (qk_norm(q, qg_ref[...]) * (FOX_DIM ** -0.5 * LOG2E)).astype(BF16)
        s_diag = lax.dot_general(q, k_ref[lo:lo + blk, :], nt, preferred_element_type=F32)
        if has_bias:
            s_diag = s_diag - d_ref[pl.ds(h, 1), lo:lo + blk]
        s_diag = jnp.where(col <= row, s_diag, -jnp.inf)
        m = jnp.max(s_diag, axis=-1, keepdims=True)
        if qi:
            s_off = lax.dot_general(q, k_ref[0:lo, :], nt, preferred_element_type=F32)
            if has_bias:
                s_off = s_off - d_ref[pl.ds(h, 1), 0:lo]
            m = jnp.maximum(m, jnp.max(s_off, axis=-1, keepdims=True))
        p_diag = jnp.exp2(s_diag - m)
        l = jnp.sum(p_diag, axis=-1, keepdims=True)
        acc = jnp.dot(p_diag.astype(BF16), v_ref[lo:lo + blk, :], preferred_element_type=F32)
        if qi:
            p_off = jnp.exp2(s_off - m)
            l = l + jnp.sum(p_off, axis=-1, keepdims=True)
            acc = acc + jnp.dot(p_off.astype(BF16), v_ref[0:lo, :], preferred_element_type=F32)
        o_ref[lo:lo + blk, :] = (acc * (1.0 / l)).astype(o_ref.dtype)


def _flash(q_arr, q_off, k_arr, k_off, v_arr, v_off, dk, n_heads, out_width, fox=None):
    b, s, _ = q_arr.shape
    blk = ATT_BLOCK
    in_specs = [
        pl.BlockSpec((None, s, dk), lambda bi, h: (bi, 0, q_off + h)),
        pl.BlockSpec((None, s, dk), lambda bi, h: (bi, 0, k_off + h)),
        pl.BlockSpec((None, s, LANES), lambda bi, h: (bi, 0, v_off + h)),
    ]
    args = [q_arr, k_arr, v_arr]
    scratch = []
    if fox is not None:
        rows, q_gain, k_gain = fox
        in_specs += [pl.BlockSpec((None, 8, s), lambda bi, h: (bi, 0, 0)),
                     pl.BlockSpec((1, dk), lambda bi, h: (0, 0)),
                     pl.BlockSpec((1, dk), lambda bi, h: (0, 0))]
        args += [rows, q_gain.reshape(1, dk), k_gain.reshape(1, dk)]
        scratch = [pltpu.VMEM((s, dk), BF16)]
    return pl.pallas_call(
        functools.partial(_flash_kernel, blk=blk, has_bias=fox is not None),
        grid=(b, n_heads),
        in_specs=in_specs,
        out_specs=pl.BlockSpec((None, s, LANES), lambda bi, h: (bi, 0, h)),
        out_shape=jax.ShapeDtypeStruct((b, s, out_width), BF16),
        scratch_shapes=scratch,
        compiler_params=_params("parallel", "parallel"),
        name="flash_bias" if fox is not None else "flash",
    )(*args)


def _mlstm_kernel(q_ref, k_ref, v_ref, og_ref, rows_ref, gain_ref, wq_ref, wk_ref, bq_ref, bk_ref, o_ref,
                  xq_ref, xk_ref, *, chunk):
    L = chunk
    hp = pl.program_id(1)
    s = q_ref.shape[0]
    nc = s // L
    for x_ref, xs_ref in ((q_ref, xq_ref), (k_ref, xk_ref)):
        xs_ref[0:PAD, :] = jnp.zeros((PAD, LANES), F32)
        for c in range(nc):
            xs_ref[PAD + c * L:PAD + (c + 1) * L, :] = x_ref[c * L:(c + 1) * L, :].astype(F32)

    def conv_silu(xs_ref, w_ref, b_ref, t0):
        y = b_ref[...]
        for j in range(ML_CONV):
            off = PAD + t0 - (ML_CONV - 1) + j
            y = y + w_ref[j:j + 1, :] * xs_ref[off:off + L, :]
        return y * _sigmoid(y)

    lane = lax.broadcasted_iota(jnp.int32, (1, LANES), 1)
    row = lax.broadcasted_iota(jnp.int32, (L, L), 0)
    col = lax.broadcasted_iota(jnp.int32, (L, L), 1)
    causal = col <= row
    eye = col == row
    col1 = lax.broadcasted_iota(jnp.int32, (1, L), 1)
    row1 = lax.broadcasted_iota(jnp.int32, (L, 1), 0)

    def to_col(r):
        return jnp.sum(jnp.where(eye, r, 0.0), axis=-1, keepdims=True)

    def head_step(t0, hh, qp, kp, kp_b, state):
        c_state, n_state, m_prev = state
        head = 2 * hp + hh
        li = rows_ref[pl.ds(8 + head, 1), pl.ds(t0, L)]
        bc = rows_ref[pl.ds(16 + head, 1), pl.ds(t0, L)]
        a_row = li - bc
        bc_col = to_col(bc)
        log_w = jnp.where(causal, bc_col + a_row, -jnp.inf)
        log_inter = bc_col + m_prev
        m_t = jnp.maximum(log_inter, jnp.max(log_w, axis=-1, keepdims=True))
        w = jnp.exp2(log_w - m_t)
        inter = jnp.exp2(log_inter - m_t)
        head_lanes = (lane >= hh * ML_QK) & (lane < (hh + 1) * ML_QK)
        qh = jnp.where(head_lanes, qp, jnp.zeros_like(qp))
        qk = lax.dot_general(qh, kp_b, (((1,), (1,)), ((), ())), preferred_element_type=F32) * w
        vh = v_ref[pl.ds(t0, L), hh * ML_V:(hh + 1) * ML_V]
        num = (jnp.dot(qk.astype(BF16), vh, preferred_element_type=F32)
               + inter * jnp.dot(qh, c_state.astype(BF16), preferred_element_type=F32))
        den = (jnp.sum(qk, axis=-1, keepdims=True)
               + inter * jnp.sum(qh.astype(F32) * n_state, axis=-1, keepdims=True))
        h_out = num * (1.0 / jnp.maximum(jnp.abs(den), jnp.exp2(-m_t)))
        hn = h_out * lax.rsqrt(jnp.mean(h_out * h_out, axis=-1, keepdims=True) + NORM_EPS)
        hn = hn * gain_ref[:, hh * ML_V:(hh + 1) * ML_V]
        og = og_ref[pl.ds(t0, L), hh * ML_V:(hh + 1) * ML_V].astype(F32)
        o_ref[pl.ds(t0, L), hh * ML_V:(hh + 1) * ML_V] = (hn * _sigmoid(og)).astype(o_ref.dtype)
        m_new = jnp.sum(jnp.where(row1 == L - 1, m_t, 0.0), axis=0, keepdims=True)
        b_last = jnp.sum(jnp.where(col1 == L - 1, bc, 0.0), axis=-1, keepdims=True)
        decay = jnp.exp2(b_last + m_prev - m_new)
        w_last = to_col(jnp.exp2(b_last + a_row - m_new))
        kw = kp * w_last
        c_upd = lax.dot_general(kw.astype(BF16), vh, (((0,), (0,)), ((), ())), preferred_element_type=F32)
        return decay * c_state + c_upd, decay * n_state + jnp.sum(kw, axis=0, keepdims=True), m_new

    states = [(jnp.zeros((LANES, ML_V), F32), jnp.zeros((1, LANES), F32), jnp.zeros((1, 1), F32))
              for _ in range(2)]
    for ci in range(nc):
        t0 = ci * L
        qp = conv_silu(xq_ref, wq_ref, bq_ref, t0).astype(BF16)
        kp = conv_silu(xk_ref, wk_ref, bk_ref, t0) * (ML_QK ** -0.5)
        kp_b = kp.astype(BF16)
        states = [head_step(t0, hh, qp, kp, kp_b, states[hh]) for hh in range(2)]


def _mlstm(qkv_arr, og_arr, rows, out_gain, conv_w, conv_b):
    b, s, _ = qkv_arr.shape
    npair = ML_HEADS // 2
    v_blk0 = 2 * ML_HEADS * ML_QK // (2 * ML_V)
    pair = lambda off: (lambda bi, hp: (0, off + hp))
    return pl.pallas_call(
        functools.partial(_mlstm_kernel, chunk=ML_CHUNK),
        grid=(b, npair),
        in_specs=[pl.BlockSpec((None, s, LANES), lambda bi, hp: (bi, 0, hp)),
                  pl.BlockSpec((None, s, LANES), lambda bi, hp: (bi, 0, npair + hp)),
                  pl.BlockSpec((None, s, 2 * ML_V), lambda bi, hp: (bi, 0, v_blk0 + hp)),
                  pl.BlockSpec((None, s, 2 * ML_V), lambda bi, hp: (bi, 0, hp)),
                  pl.BlockSpec((None, 24, s), lambda bi, hp: (bi, 0, 0)),
                  pl.BlockSpec((1, 2 * ML_V), pair(0)),
                  pl.BlockSpec((ML_CONV, LANES), pair(0)),
                  pl.BlockSpec((ML_CONV, LANES), pair(npair)),
                  pl.BlockSpec((1, LANES), pair(0)),
                  pl.BlockSpec((1, LANES), pair(npair))],
        out_specs=pl.BlockSpec((None, s, 2 * ML_V), lambda bi, hp: (bi, 0, hp)),
        out_shape=jax.ShapeDtypeStruct((b, s, ML_HEADS * ML_V), BF16),
        scratch_shapes=[pltpu.VMEM((s + PAD, LANES), F32), pltpu.VMEM((s + PAD, LANES), F32)],
        compiler_params=_params("parallel", "parallel"),
        name="mlstm",
    )(qkv_arr, qkv_arr, qkv_arr, og_arr, rows, out_gain.reshape(1, -1), conv_w, conv_w,
      conv_b.reshape(1, -1), conv_b.reshape(1, -1))


def _mla_prep_kernel(s_ref, cos_ref, sin_ref, wuq_ref, wukv_ref, cqg_ref, ckvg_ref, qgn_ref, qgr_ref,
                     kgn_ref, kgr_ref, q_ref, k_ref, v_ref):
    scale = MLA_QK ** -0.5 * LOG2E
    lane = lax.broadcasted_iota(jnp.int32, (1, LANES), 1)
    cosf = cos_ref[...]
    sinf = sin_ref[...]

    def norm(x, g):
        return x * lax.rsqrt(jnp.mean(x * x, axis=-1, keepdims=True) + NORM_EPS) * g

    def rope(x):
        swap = jnp.where(lane < MLA_ROPE // 2, pltpu.roll(x, LANES - MLA_ROPE // 2, axis=1),
                         pltpu.roll(x, MLA_ROPE // 2, axis=1))
        return x * cosf + swap * sinf

    cq = s_ref[:, 0:MLA_Q_RANK]
    ckv = s_ref[:, MLA_Q_RANK:MLA_Q_RANK + MLA_KV_RANK]
    kr = jnp.where(lane < MLA_ROPE, s_ref[:, MLA_Q_RANK + MLA_KV_RANK:], 0.0)
    qf = jnp.dot(norm(cq, cqg_ref[...]).astype(BF16), wuq_ref[...], preferred_element_type=F32)
    kvf = jnp.dot(norm(ckv, ckvg_ref[...]).astype(BF16), wukv_ref[...], preferred_element_type=F32)
    ss_kr = jnp.sum(kr * kr, axis=-1, keepdims=True)
    kr_rot = rope(kr * kgr_ref[...])
    qgn_s = qgn_ref[...] * scale
    qgr_s = qgr_ref[...] * scale
    nw = MLA_HEADS * MLA_NOPE
    for h in range(MLA_HEADS):
        qn = qf[:, h * MLA_NOPE:(h + 1) * MLA_NOPE]
        qr = qf[:, nw + h * LANES:nw + (h + 1) * LANES]
        ss = jnp.sum(qn * qn, axis=-1, keepdims=True) + jnp.sum(qr * qr, axis=-1, keepdims=True)
        r = lax.rsqrt(ss / MLA_QK + NORM_EPS)
        q_ref[:, h * MLA_PAD:h * MLA_PAD + MLA_NOPE] = ((qn * r) * qgn_s).astype(q_ref.dtype)
        q_ref[:, h * MLA_PAD + MLA_NOPE:(h + 1) * MLA_PAD] = rope((qr * r) * qgr_s).astype(q_ref.dtype)
        kn = kvf[:, h * 2 * MLA_NOPE:h * 2 * MLA_NOPE + MLA_NOPE]
        rk = lax.rsqrt((jnp.sum(kn * kn, axis=-1, keepdims=True) + ss_kr) / MLA_QK + NORM_EPS)
        k_ref[:, h * MLA_PAD:h * MLA_PAD + MLA_NOPE] = ((kn * rk) * kgn_ref[...]).astype(k_ref.dtype)
        k_ref[:, h * MLA_PAD + MLA_NOPE:(h + 1) * MLA_PAD] = (kr_rot * rk).astype(k_ref.dtype)
        v_ref[:, h * MLA_V:(h + 1) * MLA_V] = kvf[:, h * 2 * MLA_NOPE + MLA_NOPE:(h + 1) * 2 * MLA_NOPE].astype(v_ref.dtype)


def _mla_prep(seg3, cosf, sinf, wuq_p, wukv_b, cq_gain, ckv_gain, q_gain, k_gain, tm=256):
    m, w3 = seg3.shape
    pad = jnp.zeros((LANES - MLA_ROPE,), F32)
    qgn = q_gain[:MLA_NOPE].reshape(1, -1)
    qgr = jnp.concatenate([q_gain[MLA_NOPE:], pad]).reshape(1, -1)
    kgn = k_gain[:MLA_NOPE].reshape(1, -1)
    kgr = jnp.concatenate([k_gain[MLA_NOPE:], pad]).reshape(1, -1)
    const = lambda i: (0, 0)
    return pl.pallas_call(
        _mla_prep_kernel,
        grid=(m // tm,),
        in_specs=[pl.BlockSpec((tm, w3), lambda i: (i, 0)),
                  pl.BlockSpec((tm, LANES), lambda i: (i, 0)),
                  pl.BlockSpec((tm, LANES), lambda i: (i, 0)),
                  pl.BlockSpec(wuq_p.shape, const),
                  pl.BlockSpec(wukv_b.shape, const),
                  pl.BlockSpec((1, MLA_Q_RANK), const),
                  pl.BlockSpec((1, MLA_KV_RANK), const),
                  pl.BlockSpec((1, LANES), const),
                  pl.BlockSpec((1, LANES), const),
                  pl.BlockSpec((1, LANES), const),
                  pl.BlockSpec((1, LANES), const)],
        out_specs=[pl.BlockSpec((tm, MLA_HEADS * MLA_PAD), lambda i: (i, 0)),
                   pl.BlockSpec((tm, MLA_HEADS * MLA_PAD), lambda i: (i, 0)),
                   pl.BlockSpec((tm, MLA_HEADS * MLA_V), lambda i: (i, 0))],
        out_shape=[jax.ShapeDtypeStruct((m, MLA_HEADS * MLA_PAD), BF16),
                   jax.ShapeDtypeStruct((m, MLA_HEADS * MLA_PAD), BF16),
                   jax.ShapeDtypeStruct((m, MLA_HEADS * MLA_V), BF16)],
        compiler_params=_params("parallel"),
        name="mla_prep",
    )(seg3, cosf, sinf, wuq_p, wukv_b, cq_gain.reshape(1, -1), ckv_gain.reshape(1, -1), qgn, qgr, kgn, kgr)


_FW = FOX_HEADS * FOX_DIM
_IN_SIZES = (_FW, _FW, _FW, FOX_HEADS, ML_HEADS * ML_QK, ML_HEADS * ML_QK, ML_HEADS * ML_V,
             ML_HEADS, ML_HEADS, ML_HEADS * ML_V, MLA_Q_RANK, MLA_KV_RANK, MLA_ROPE)
_IN_OFFS = [sum(_IN_SIZES[:i]) for i in range(len(_IN_SIZES) + 1)]
(_C_FQ, _, _, _C_FF, _C_MQ, _, _, _C_MI, _C_MF, _C_MO, _C_CQ, _, _, _C_GATES) = _IN_OFFS


def _small_gate_weight(w_in_t, l):
    d = w_in_t.shape[2]
    rows = [jnp.zeros((MLA_ROPE, d), w_in_t.dtype),
            w_in_t[l, _C_FF:_C_FF + FOX_HEADS],
            w_in_t[l, _C_MI:_C_MI + ML_HEADS],
            w_in_t[l, _C_MF:_C_MF + ML_HEADS],
            jnp.zeros((LANES - MLA_ROPE - FOX_HEADS - 2 * ML_HEADS, d), w_in_t.dtype)]
    return jnp.concatenate(rows, axis=0)


def _layer(x, cosf, sinf, b, s, p, big, l):
    m, d = x.shape
    h1, g_pre = _rmsnorm(x, p["mix_norm"], _small_gate_weight(big["w_in_t"], l))
    w_in = (big["w_in_t"], l)

    def in_proj(first_col, width, out_dtype, name, extras=(), epilogue=_ep_plain, tn=1024):
        return _ws_matmul([(h1, d, 0)], [(w_in, d, 0, 0)], list(extras), epilogue, tm=1024, tn=tn, n_out=width,
                          out_dtype=out_dtype, first_row=first_col, name=name)

    proj_fox = in_proj(_C_FQ, 3 * _FW, BF16, "proj_fox")
    proj_ml = in_proj(_C_MQ, 2 * ML_HEADS * ML_QK + ML_HEADS * ML_V, BF16, "proj_ml")
    proj_mo = in_proj(_C_MO, ML_HEADS * ML_V, BF16, "proj_mo")
    w_mla = MLA_Q_RANK + MLA_KV_RANK + LANES
    seg3 = in_proj(_C_CQ, w_mla, F32, "proj_mla", tn=w_mla // 2)
    gates = in_proj(_C_GATES, 3 * d, BF16, "gates", [(p["gate_bias"].reshape(1, -1), "row", 0)], _ep_bias_sigmoid)

    bias_vec = jnp.concatenate([jnp.zeros((MLA_ROPE,), F32), p["fox_f_bias"], p["mlstm_i_bias"], p["mlstm_f_bias"],
                                jnp.zeros((LANES - MLA_ROPE - 24,), F32)]).reshape(1, LANES)
    rows = _gate_rows(g_pre, bias_vec, b, s)

    proj_fox_3d = proj_fox.reshape(b, s, -1)
    o_fox = _flash(proj_fox_3d, 0, proj_fox_3d, FOX_HEADS, proj_fox_3d, 2 * FOX_HEADS, FOX_DIM, FOX_HEADS,
                   FOX_HEADS * FOX_DIM, fox=(rows, p["fox_q_gain"], p["fox_k_gain"])).reshape(m, -1)
    o_ml = _mlstm(proj_ml.reshape(b, s, -1), proj_mo.reshape(b, s, -1), rows, p["mlstm_out_gain"],
                  p["mlstm_conv_w"], p["mlstm_conv_b"]).reshape(m, -1)
    wuq = p["mla_w_uq"].reshape(MLA_Q_RANK, MLA_HEADS, MLA_QK)
    wuq_p = jnp.concatenate([
        wuq[:, :, :MLA_NOPE].reshape(MLA_Q_RANK, -1),
        jnp.pad(wuq[:, :, MLA_NOPE:], ((0, 0), (0, 0), (0, LANES - MLA_ROPE))).reshape(MLA_Q_RANK, -1)],
        axis=1).astype(BF16)
    q_mla, k_mla, v_mla = _mla_prep(seg3, cosf, sinf, wuq_p, p["mla_w_ukv"].astype(BF16), p["mla_cq_gain"],
                                    p["mla_ckv_gain"], p["mla_q_gain"], p["mla_k_gain"])
    o_mla = _flash(q_mla.reshape(b, s, -1), 0, k_mla.reshape(b, s, -1), 0, v_mla.reshape(b, s, -1), 0,
                   MLA_PAD, MLA_HEADS, MLA_HEADS * MLA_V).reshape(m, -1)

    kb = o_fox.shape[1]
    tn_merge = 1024
    nb = d // tn_merge
    merged = _ws_matmul([(o_fox, kb, 0), (o_ml, kb, 0), (o_mla, kb, 0)],
                        [((big["w_fox_out"], l), kb, 0, 0), ((big["w_mlstm_out"], l), kb, 0, 0),
                         ((big["w_mla_out"], l), kb, 0, 0)],
                        [(gates, "tile", 0), (gates, "tile", nb), (gates, "tile", 2 * nb)],
                        _ep_merge, tm=1024, tn=tn_merge, n_out=d, out_dtype=BF16, name="merge")
    x = _ws_matmul([(merged, d, 0)], [((big["w_o"], l), d, 0, 0)], [(x, "tile", 0)], _ep_residual,
                   tm=1024, tn=512, n_out=d, out_dtype=F32, name="w_o")

    h2 = _rmsnorm(x, p["ffn_norm"])
    d_ff = big["w_gate"].shape[2]
    act = _ws_matmul([(h2, d, 0)], [((big["w_gate"], l), d, 0, 0), ((big["w_up"], l), d, 0, 0)], [], _ep_swiglu,
                     tm=2048, tn=256, n_out=d_ff, out_dtype=BF16, pairs=[(0, 0), (0, 1)], name="gate_up")
    k_first = pl.cdiv(d_ff // MXU_DEPTH, 2) * MXU_DEPTH
    for ki, (k0, kb) in enumerate(((0, k_first), (k_first, d_ff - k_first))):
        x = _ws_matmul([(act, kb, k0)], [((big["w_down"], l), kb, k0, 0)], [(x, "tile", 0)], _ep_residual,
                       tm=1024, tn=512, n_out=d, out_dtype=F32, name=f"down{ki}")
    return x


def kernel(x, positions, mix_norm, w_in, fox_f_bias, fox_q_gain, fox_k_gain, mlstm_conv_w, mlstm_conv_b,
           mlstm_i_bias, mlstm_f_bias, mlstm_out_gain, mla_cq_gain, mla_ckv_gain, mla_w_uq, mla_w_ukv,
           mla_q_gain, mla_k_gain, w_fox_out, w_mlstm_out, w_mla_out, gate_bias, w_o, ffn_norm, w_gate,
           w_up, w_down):
    b, s, d = x.shape
    depth = w_in.shape[0]
    inv_freq = jnp.power(ROPE_THETA, -jnp.arange(0, MLA_ROPE, 2, dtype=F32) / MLA_ROPE)
    ang = positions.astype(F32).reshape(b * s, 1) * inv_freq
    zpad = jnp.zeros((b * s, LANES - MLA_ROPE), F32)
    cosf = jnp.concatenate([jnp.cos(ang), jnp.cos(ang), zpad], axis=1)
    sinf = jnp.concatenate([-jnp.sin(ang), jnp.sin(ang), zpad], axis=1)
    names = dict(mix_norm=mix_norm, fox_f_bias=fox_f_bias, fox_q_gain=fox_q_gain, fox_k_gain=fox_k_gain,
                 mlstm_conv_w=mlstm_conv_w, mlstm_conv_b=mlstm_conv_b, mlstm_i_bias=mlstm_i_bias,
                 mlstm_f_bias=mlstm_f_bias, mlstm_out_gain=mlstm_out_gain, mla_cq_gain=mla_cq_gain,
                 mla_ckv_gain=mla_ckv_gain, mla_w_uq=mla_w_uq, mla_w_ukv=mla_w_ukv, mla_q_gain=mla_q_gain,
                 mla_k_gain=mla_k_gain, gate_bias=gate_bias, ffn_norm=ffn_norm)
    big = dict(w_in_t=jnp.swapaxes(w_in, 1, 2), w_fox_out=w_fox_out, w_mlstm_out=w_mlstm_out, w_mla_out=w_mla_out, w_o=w_o, w_gate=w_gate,
               w_up=w_up, w_down=w_down)
    xf = x.reshape(b * s, d)
    for l in range(depth):
        xf = _layer(xf, cosf, sinf, b, s, {k: v[l] for k, v in names.items()}, big, l)
    return xf.reshape(b, s, d)
```

```python
import functools

import jax
import jax.numpy as jnp
from jax import lax
from jax.experimental import pallas as pl
from jax.experimental.pallas import tpu as pltpu

F32 = jnp.float32
BF16 = jnp.bfloat16

NORM_EPS = 1e-6
ROPE_THETA = 10000.0
LOG2E = 1.4426950408889634

FOX_HEADS = 8
FOX_DIM = 128
ML_HEADS = 8
ML_QK = 64
ML_V = 128
ML_CONV = 4
MLA_HEADS = 8
MLA_Q_RANK = 896
MLA_KV_RANK = 512
MLA_NOPE = 128
MLA_ROPE = 64
MLA_V = 128
MLA_QK = MLA_NOPE + MLA_ROPE
MLA_PAD = 256

LANES = 128
SUBLANES = 8
MXU_DEPTH = 256
VMEM_LIMIT = 56 * 1024 * 1024
ML_CHUNK = 512
ATT_BLOCK = 256
PAD = 8


def _params(*sem):
    return pltpu.CompilerParams(dimension_semantics=sem, vmem_limit_bytes=VMEM_LIMIT)


def _rmsnorm_kernel(x_ref, g_ref, *rest):
    x = x_ref[...]
    ms = jnp.mean(x * x, axis=-1, keepdims=True)
    h = (x * lax.rsqrt(ms + NORM_EPS) * g_ref[...]).astype(BF16)
    if len(rest) == 1:
        rest[0][...] = h
    else:
        w_ref, o_ref, gp_ref = rest
        o_ref[...] = h
        gp_ref[...] = lax.dot_general(h, w_ref[...].astype(BF16), (((1,), (1,)), ((), ())),
                                      preferred_element_type=F32)


def _rmsnorm(x, gain, w_small_t=None, tm=256):
    m, d = x.shape
    in_specs = [pl.BlockSpec((tm, d), lambda i: (i, 0)),
                pl.BlockSpec((1, d), lambda i: (0, 0))]
    out_specs = pl.BlockSpec((tm, d), lambda i: (i, 0))
    out_shape = jax.ShapeDtypeStruct((m, d), BF16)
    args = [x, gain.reshape(1, d)]
    if w_small_t is not None:
        in_specs.append(pl.BlockSpec(w_small_t.shape, lambda i: (0, 0)))
        out_specs = [out_specs, pl.BlockSpec((tm, LANES), lambda i: (i, 0))]
        out_shape = [out_shape, jax.ShapeDtypeStruct((m, LANES), F32)]
        args.append(w_small_t)
    return pl.pallas_call(
        _rmsnorm_kernel,
        grid=(m // tm,),
        in_specs=in_specs,
        out_specs=out_specs,
        out_shape=out_shape,
        compiler_params=_params("parallel"),
        name="rmsnorm",
    )(*args)


def _ws_kernel(*refs, nx, nw, ne, pairs, epilogue, cast_rows, wt, panel_src):
    x_refs = refs[:nx]
    w_hbm = refs[nx:nx + nw]
    e_refs = refs[nx + nw:nx + nw + ne]
    o_ref = refs[nx + nw + ne]
    scratch = refs[nx + nw + ne + 1:]
    stage, w_bf, sem = scratch[:nw], scratch[nw:2 * nw], scratch[2 * nw]
    n = pl.program_id(0)
    n_panels = pl.num_programs(0)

    def panel_copy(wi, panel):
        return pltpu.make_async_copy(panel_src[wi](w_hbm[wi], panel), stage[wi], sem.at[wi])

    @pl.when(pl.program_id(1) == 0)
    def _():
        @pl.when(n == 0)
        def _():
            for wi in range(nw):
                panel_copy(wi, 0).start()

        for wi in range(nw):
            panel_copy(wi, n).wait()

            def body(i, c, wi=wi):
                r = pl.multiple_of(i * cast_rows, cast_rows)
                w_bf[wi][pl.ds(r, cast_rows), :] = stage[wi][pl.ds(r, cast_rows), :].astype(BF16)
                return c

            lax.fori_loop(0, stage[wi].shape[0] // cast_rows, body, 0)

        @pl.when(n + 1 < n_panels)
        def _():
            for wi in range(nw):
                panel_copy(wi, n + 1).start()

    dims = (((1,), (1 if wt else 0,)), ((), ()))
    dots = [lax.dot_general(x_refs[xi][...], w_bf[wi][...], dims, preferred_element_type=F32) for xi, wi in pairs]
    o_ref[...] = epilogue(dots, [e[...] for e in e_refs]).astype(o_ref.dtype)


def _ws_matmul(xs, ws, extras, epilogue, *, tm, tn, n_out, out_dtype, pairs=None, wt=False, first_row=None, name):
    m = xs[0][0].shape[0]
    assert n_out % tn == 0 and m % tm == 0
    pairs = pairs or [(i, i) for i in range(len(ws))]
    wt = wt or first_row is not None
    in_specs = []
    args = []
    panel_src = []
    for a, kb, k0 in xs:
        if k0 % kb == 0:
            in_specs.append(pl.BlockSpec((tm, kb), lambda n, i, kblk=k0 // kb: (i, kblk)))
        else:
            assert k0 % LANES == 0 and k0 + kb <= a.shape[1]
            in_specs.append(pl.BlockSpec((pl.Element(tm), pl.Element(kb)), lambda n, i, k0=k0: (i * tm, k0)))
        args.append(a)
    for a, kb, k0, off in ws:
        layer = None
        if isinstance(a, tuple):
            a, layer = a
        assert a.dtype == F32
        if first_row is not None:
            assert len(ws) == 1 and kb == a.shape[2] and first_row % SUBLANES == 0 and tn % SUBLANES == 0
            assert first_row + n_out <= a.shape[1]
            panel_src.append(lambda w, p, layer=layer: w.at[
                layer, pl.ds(pl.multiple_of(first_row + p * tn, SUBLANES), tn), :])
        elif wt:
            assert layer is None and kb == a.shape[1]
            panel_src.append(lambda w, p: w.at[pl.ds(pl.multiple_of(p * tn, SUBLANES), tn), :])
        else:
            assert layer is not None
            panel_src.append(lambda w, p, layer=layer, kb=kb, k0=k0, off=off: w.at[
                layer, pl.ds(k0, kb), pl.ds(pl.multiple_of((p + off) * tn, LANES), tn)])
        in_specs.append(pl.BlockSpec(memory_space=pl.ANY))
        args.append(a)
    for a, kind, off in extras:
        if kind == "tile":
            in_specs.append(pl.BlockSpec((tm, tn), lambda n, i, off=off: (i, n + off)))
        else:
            in_specs.append(pl.BlockSpec((1, tn), lambda n, i, off=off: (0, n + off)))
        args.append(a)
    panel_shapes = [(tn, kb) if wt else (kb, tn) for _, kb, _, _ in ws]
    cast_rows = LANES
    for shp in panel_shapes:
        assert shp[0] % cast_rows == 0, shp
    scratch = ([pltpu.VMEM(shp, F32) for shp in panel_shapes] + [pltpu.VMEM(shp, BF16) for shp in panel_shapes]
               + [pltpu.SemaphoreType.DMA((len(ws),))])
    kern = functools.partial(_ws_kernel, nx=len(xs), nw=len(ws), ne=len(extras), pairs=pairs, epilogue=epilogue,
                             cast_rows=cast_rows, wt=wt, panel_src=panel_src)
    return pl.pallas_call(
        kern,
        grid=(n_out // tn, m // tm),
        in_specs=in_specs,
        out_specs=pl.BlockSpec((tm, tn), lambda n, i: (i, n)),
        out_shape=jax.ShapeDtypeStruct((m, n_out), out_dtype),
        scratch_shapes=scratch,
        compiler_params=_params("arbitrary", "arbitrary"),
        name=name,
    )(*args)


def _ep_plain(dots, extras):
    return dots[0]


def _sigmoid(x):
    return 0.5 * jnp.tanh(0.5 * x) + 0.5


def _ep_bias_sigmoid(dots, extras):
    return _sigmoid(dots[0] + extras[0])


def _ep_residual(dots, extras):
    return extras[0] + dots[0]


def _ep_swiglu(dots, extras):
    g, u = dots
    return g * _sigmoid(g) * u


def _ep_merge(dots, extras):
    return (extras[0].astype(F32) * dots[0] + extras[1].astype(F32) * dots[1]
            + extras[2].astype(F32) * dots[2])


def _log_sigmoid(x):
    return jnp.minimum(x, 0.0) - jnp.log1p(jnp.exp(-jnp.abs(x)))


def _cumsum_lanes(x, seg):
    n = x.shape[-1]
    pos = lax.broadcasted_iota(jnp.int32, x.shape, 1) % seg
    shift = 1
    while shift < seg:
        x = x + jnp.where(pos >= shift, pltpu.roll(x, shift, axis=1), 0.0)
        shift *= 2
    del n
    return x


def _gates_kernel(g_ref, bias_ref, rows_ref, *, chunk):
    g = g_ref[...] + bias_ref[...]
    gt = g.T
    s = gt.shape[1]
    rows_ref[0:8, :] = _cumsum_lanes(_log_sigmoid(gt[64:72, :]), s) * LOG2E
    rows_ref[8:16, :] = gt[72:80, :] * LOG2E
    rows_ref[16:24, :] = _cumsum_lanes(_log_sigmoid(gt[80:88, :]), chunk) * LOG2E


def _gate_rows(g_pre, bias_vec, b, s):
    return pl.pallas_call(
        functools.partial(_gates_kernel, chunk=ML_CHUNK),
        grid=(b,),
        in_specs=[pl.BlockSpec((s, LANES), lambda i: (i, 0)),
                  pl.BlockSpec((1, LANES), lambda i: (0, 0))],
        out_specs=pl.BlockSpec((None, 24, s), lambda i: (i, 0, 0)),
        out_shape=jax.ShapeDtypeStruct((b, 24, s), F32),
        compiler_params=_params("parallel"),
        name="gate_rows",
    )(g_pre, bias_vec)


def _flash_kernel(*refs, blk, has_bias):
    if has_bias:
        q_ref, kraw_ref, v_ref, d_ref, qg_ref, kg_ref, o_ref, k_ref = refs
    else:
        q_ref, k_ref, v_ref, o_ref = refs
        d_ref = None
    h = pl.program_id(1)
    s_len = k_ref.shape[0]
    row = lax.broadcasted_iota(jnp.int32, (blk, blk), 0)
    col = lax.broadcasted_iota(jnp.int32, (blk, blk), 1)
    nt = (((1,), (1,)), ((), ()))

    def qk_norm(x, gain):
        xf = x.astype(F32)
        return xf * lax.rsqrt(jnp.mean(xf * xf, axis=-1, keepdims=True) + NORM_EPS) * gain

    if has_bias:
        for c in range(s_len // blk):
            k_ref[c * blk:(c + 1) * blk, :] = qk_norm(kraw_ref[c * blk:(c + 1) * blk, :], kg_ref[...]).astype(BF16)
    for qi in range(s_len // blk):
        lo = qi * blk
        q = q_ref[lo:lo + blk, :]
        if has_bias:
            q = (qk_norm(q, qg_ref[...]) * (FOX_DIM ** -0.5 * LOG2E)).astype(BF16)
        s_diag = lax.dot_general(q, k_ref[lo:lo + blk, :], nt, preferred_element_type=F32)
        if has_bias:
            s_diag = s_diag - d_ref[pl.ds(h, 1), lo:lo + blk]
        s_diag = jnp.where(col <= row, s_diag, -jnp.inf)
        m = jnp.max(s_diag, axis=-1, keepdims=True)
        if qi:
            s_off = lax.dot_general(q, k_ref[0:lo, :], nt, preferred_element_type=F32)
            if has_bias:
                s_off = s_off - d_ref[pl.ds(h, 1), 0:lo]
            m = jnp.maximum(m, jnp.max(s_off, axis=-1, keepdims=True))
        p_diag = jnp.exp2(s_diag - m)
        l = jnp.sum(p_diag, axis=-1, keepdims=True)
        acc = jnp.dot(p_diag.astype(BF16), v_ref[lo:lo + blk, :], preferred_element_type=F32)
        if qi:
            p_off = jnp.exp2(s_off - m)
            l = l + jnp.sum(p_off, axis=-1, keepdims=True)
            acc = acc + jnp.dot(p_off.astype(BF16), v_ref[0:lo, :], preferred_element_type=F32)
        o_ref[lo:lo + blk, :] = (acc * (1.0 / l)).astype(o_ref.dtype)


def _flash(q_arr, q_off, k_arr, k_off, v_arr, v_off, dk, n_heads, out_width, fox=None):
    b, s, _ = q_arr.shape
    blk = ATT_BLOCK
    in_specs = [
        pl.BlockSpec((None, s, dk), lambda bi, h: (bi, 0, q_off + h)),
        pl.BlockSpec((None, s, dk), lambda bi, h: (bi, 0, k_off + h)),
        pl.BlockSpec((None, s, LANES), lambda bi, h: (bi, 0, v_off + h)),
    ]
    args = [q_arr, k_arr, v_arr]
    scratch = []
    if fox is not None:
        rows, q_gain, k_gain = fox
        in_specs += [pl.BlockSpec((None, 8, s), lambda bi, h: (bi, 0, 0)),
                     pl.BlockSpec((1, dk), lambda bi, h: (0, 0)),
                     pl.BlockSpec((1, dk), lambda bi, h: (0, 0))]
        args += [rows, q_gain.reshape(1, dk), k_gain.reshape(1, dk)]
        scratch = [pltpu.VMEM((s, dk), BF16)]
    return pl.pallas_call(
        functools.partial(_flash_kernel, blk=blk, has_bias=fox is not None),
        grid=(b, n_heads),
        in_specs=in_specs,
        out_specs=pl.BlockSpec((None, s, LANES), lambda bi, h: (bi, 0, h)),
        out_shape=jax.ShapeDtypeStruct((b, s, out_width), BF16),
        scratch_shapes=scratch,
        compiler_params=_params("parallel", "parallel"),
        name="flash_bias" if fox is not None else "flash",
    )(*args)


def _mlstm_kernel(q_ref, k_ref, v_ref, og_ref, rows_ref, gain_ref, wq_ref, wk_ref, bq_ref, bk_ref, o_ref,
                  xq_ref, xk_ref, *, chunk):
    L = chunk
    hp = pl.program_id(1)
    s = q_ref.shape[0]
    nc = s // L
    for x_ref, xs_ref in ((q_ref, xq_ref), (k_ref, xk_ref)):
        xs_ref[0:PAD, :] = jnp.zeros((PAD, LANES), F32)
        for c in range(nc):
            xs_ref[PAD + c * L:PAD + (c + 1) * L, :] = x_ref[c * L:(c + 1) * L, :].astype(F32)

    def conv_silu(xs_ref, w_ref, b_ref, t0):
        y = b_ref[...]
        for j in range(ML_CONV):
            off = PAD + t0 - (ML_CONV - 1) + j
            y = y + w_ref[j:j + 1, :] * xs_ref[off:off + L, :]
        return y * _sigmoid(y)

    lane = lax.broadcasted_iota(jnp.int32, (1, LANES), 1)
    row = lax.broadcasted_iota(jnp.int32, (L, L), 0)
    col = lax.broadcasted_iota(jnp.int32, (L, L), 1)
    causal = col <= row
    eye = col == row
    col1 = lax.broadcasted_iota(jnp.int32, (1, L), 1)
    row1 = lax.broadcasted_iota(jnp.int32, (L, 1), 0)

    def to_col(r):
        return jnp.sum(jnp.where(eye, r, 0.0), axis=-1, keepdims=True)

    def head_step(t0, hh, qp, kp, kp_b, state):
        c_state, n_state, m_prev = state
        head = 2 * hp + hh
        li = rows_ref[pl.ds(8 + head, 1), pl.ds(t0, L)]
        bc = rows_ref[pl.ds(16 + head, 1), pl.ds(t0, L)]
        a_row = li - bc
        bc_col = to_col(bc)
        log_w = jnp.where(causal, bc_col + a_row, -jnp.inf)
        log_inter = bc_col + m_prev
        m_t = jnp.maximum(log_inter, jnp.max(log_w, axis=-1, keepdims=True))
        w = jnp.exp2(log_w - m_t)
        inter = jnp.exp2(log_inter - m_t)
        head_lanes = (lane >= hh * ML_QK) & (lane < (hh + 1) * ML_QK)
        qh = jnp.where(head_lanes, qp, jnp.zeros_like(qp))
        qk = lax.dot_general(qh, kp_b, (((1,), (1,)), ((), ())), preferred_element_type=F32) * w
        vh = v_ref[pl.ds(t0, L), hh * ML_V:(hh + 1) * ML_V]
        num = (jnp.dot(qk.astype(BF16), vh, preferred_element_type=F32)
               + inter * jnp.dot(qh, c_state.astype(BF16), preferred_element_type=F32))
        den = (jnp.sum(qk, axis=-1, keepdims=True)
               + inter * jnp.sum(qh.astype(F32) * n_state, axis=-1, keepdims=True))
        h_out = num * (1.0 / jnp.maximum(jnp.abs(den), jnp.exp2(-m_t)))
        hn = h_out * lax.rsqrt(jnp.mean(h_out * h_out, axis=-1, keepdims=True) + NORM_EPS)
        hn = hn * gain_ref[:, hh * ML_V:(hh + 1) * ML_V]
        og = og_ref[pl.ds(t0, L), hh * ML_V:(hh + 1) * ML_V].astype(F32)
        o_ref[pl.ds(t0, L), hh * ML_V:(hh + 1) * ML_V] = (hn * _sigmoid(og)).astype(o_ref.dtype)
        m_new = jnp.sum(jnp.where(row1 == L - 1, m_t, 0.0), axis=0, keepdims=True)
        b_last = jnp.sum(jnp.where(col1 == L - 1, bc, 0.0), axis=-1, keepdims=True)
        decay = jnp.exp2(b_last + m_prev - m_new)
        w_last = to_col(jnp.exp2(b_last + a_row - m_new))
        kw = kp * w_last
        c_upd = lax.dot_general(kw.astype(BF16), vh, (((0,), (0,)), ((), ())), preferred_element_type=F32)
        return decay * c_state + c_upd, decay * n_state + jnp.sum(kw, axis=0, keepdims=True), m_new

    states = [(jnp.zeros((LANES, ML_V), F32), jnp.zeros((1, LANES), F32), jnp.zeros((1, 1), F32))
              for _ in range(2)]
    for ci in range(nc):
        t0 = ci * L
        qp = conv_silu(xq_ref, wq_ref, bq_ref, t0).astype(BF16)
        kp = conv_silu(xk_ref, wk_ref, bk_ref, t0) * (ML_QK ** -0.5)
        kp_b = kp.astype(BF16)
        states = [head_step(t0, hh, qp, kp, kp_b, states[hh]) for hh in range(2)]


def _mlstm(qkv_arr, og_arr, rows, out_gain, conv_w, conv_b):
    b, s, _ = qkv_arr.shape
    npair = ML_HEADS // 2
    v_blk0 = 2 * ML_HEADS * ML_QK // (2 * ML_V)
    pair = lambda off: (lambda bi, hp: (0, off + hp))
    return pl.pallas_call(
        functools.partial(_mlstm_kernel, chunk=ML_CHUNK),
        grid=(b, npair),
        in_specs=[pl.BlockSpec((None, s, LANES), lambda bi, hp: (bi, 0, hp)),
                  pl.BlockSpec((None, s, LANES), lambda bi, hp: (bi, 0, npair + hp)),
                  pl.BlockSpec((None, s, 2 * ML_V), lambda bi, hp: (bi, 0, v_blk0 + hp)),
                  pl.BlockSpec((None, s, 2 * ML_V), lambda bi, hp: (bi, 0, hp)),
                  pl.BlockSpec((None, 24, s), lambda bi, hp: (bi, 0, 0)),
                  pl.BlockSpec((1, 2 * ML_V), pair(0)),
                  pl.BlockSpec((ML_CONV, LANES), pair(0)),
                  pl.BlockSpec((ML_CONV, LANES), pair(npair)),
                  pl.BlockSpec((1, LANES), pair(0)),
                  pl.BlockSpec((1, LANES), pair(npair))],
        out_specs=pl.BlockSpec((None, s, 2 * ML_V), lambda bi, hp: (bi, 0, hp)),
        out_shape=jax.ShapeDtypeStruct((b, s, ML_HEADS * ML_V), BF16),
        scratch_shapes=[pltpu.VMEM((s + PAD, LANES), F32), pltpu.VMEM((s + PAD, LANES), F32)],
        compiler_params=_params("parallel", "parallel"),
        name="mlstm",
    )(qkv_arr, qkv_arr, qkv_arr, og_arr, rows, out_gain.reshape(1, -1), conv_w, conv_w,
      conv_b.reshape(1, -1), conv_b.reshape(1, -1))


def _mla_prep_kernel(s_ref, cos_ref, sin_ref, wuq_ref, wukv_ref, cqg_ref, ckvg_ref, qgn_ref, qgr_ref,
                     kgn_ref, kgr_ref, q_ref, k_ref, v_ref):
    scale = MLA_QK ** -0.5 * LOG2E
    lane = lax.broadcasted_iota(jnp.int32, (1, LANES), 1)
    cosf = cos_ref[...]
    sinf = sin_ref[...]

    def norm(x, g):
        return x * lax.rsqrt(jnp.mean(x * x, axis=-1, keepdims=True) + NORM_EPS) * g

    def rope(x):
        swap = jnp.where(lane < MLA_ROPE // 2, pltpu.roll(x, LANES - MLA_ROPE // 2, axis=1),
                         pltpu.roll(x, MLA_ROPE // 2, axis=1))
        return x * cosf + swap * sinf

    cq = s_ref[:, 0:MLA_Q_RANK]
    ckv = s_ref[:, MLA_Q_RANK:MLA_Q_RANK + MLA_KV_RANK]
    kr = jnp.where(lane < MLA_ROPE, s_ref[:, MLA_Q_RANK + MLA_KV_RANK:], 0.0)
    qf = jnp.dot(norm(cq, cqg_ref[...]).astype(BF16), wuq_ref[...], preferred_element_type=F32)
    kvf = jnp.dot(norm(ckv, ckvg_ref[...]).astype(BF16), wukv_ref[...], preferred_element_type=F32)
    ss_kr = jnp.sum(kr * kr, axis=-1, keepdims=True)
    kr_rot = rope(kr * kgr_ref[...])
    qgn_s = qgn_ref[...] * scale
    qgr_s = qgr_ref[...] * scale
    nw = MLA_HEADS * MLA_NOPE
    for h in range(MLA_HEADS):
        qn = qf[:, h * MLA_NOPE:(h + 1) * MLA_NOPE]
        qr = qf[:, nw + h * LANES:nw + (h + 1) * LANES]
        ss = jnp.sum(qn * qn, axis=-1, keepdims=True) + jnp.sum(qr * qr, axis=-1, keepdims=True)
        r = lax.rsqrt(ss / MLA_QK + NORM_EPS)
        q_ref[:, h * MLA_PAD:h * MLA_PAD + MLA_NOPE] = ((qn * r) * qgn_s).astype(q_ref.dtype)
        q_ref[:, h * MLA_PAD + MLA_NOPE:(h + 1) * MLA_PAD] = rope((qr * r) * qgr_s).astype(q_ref.dtype)
        kn = kvf[:, h * 2 * MLA_NOPE:h * 2 * MLA_NOPE + MLA_NOPE]
        rk = lax.rsqrt((jnp.sum(kn * kn, axis=-1, keepdims=True) + ss_kr) / MLA_QK + NORM_EPS)
        k_ref[:, h * MLA_PAD:h * MLA_PAD + MLA_NOPE] = ((kn * rk) * kgn_ref[...]).astype(k_ref.dtype)
        k_ref[:, h * MLA_PAD + MLA_NOPE:(h + 1) * MLA_PAD] = (kr_rot * rk).astype(k_ref.dtype)
        v_ref[:, h * MLA_V:(h + 1) * MLA_V] = kvf[:, h * 2 * MLA_NOPE + MLA_NOPE:(h + 1) * 2 * MLA_NOPE].astype(v_ref.dtype)


def _mla_prep(seg3, cosf, sinf, wuq_p, wukv_b, cq_gain, ckv_gain, q_gain, k_gain, tm=256):
    m, w3 = seg3.shape
    pad = jnp.zeros((LANES - MLA_ROPE,), F32)
    qgn = q_gain[:MLA_NOPE].reshape(1, -1)
    qgr = jnp.concatenate([q_gain[MLA_NOPE:], pad]).reshape(1, -1)
    kgn = k_gain[:MLA_NOPE].reshape(1, -1)
    kgr = jnp.concatenate([k_gain[MLA_NOPE:], pad]).reshape(1, -1)
    const = lambda i: (0, 0)
    return pl.pallas_call(
        _mla_prep_kernel,
        grid=(m // tm,),
        in_specs=[pl.BlockSpec((tm, w3), lambda i: (i, 0)),
                  pl.BlockSpec((tm, LANES), lambda i: (i, 0)),
                  pl.BlockSpec((tm, LANES), lambda i: (i, 0)),
                  pl.BlockSpec(wuq_p.shape, const),
                  pl.BlockSpec(wukv_b.shape, const),
                  pl.BlockSpec((1, MLA_Q_RANK), const),
                  pl.BlockSpec((1, MLA_KV_RANK), const),
                  pl.BlockSpec((1, LANES), const),
                  pl.BlockSpec((1, LANES), const),
                  pl.BlockSpec((1, LANES), const),
                  pl.BlockSpec((1, LANES), const)],
        out_specs=[pl.BlockSpec((tm, MLA_HEADS * MLA_PAD), lambda i: (i, 0)),
                   pl.BlockSpec((tm, MLA_HEADS * MLA_PAD), lambda i: (i, 0)),
                   pl.BlockSpec((tm, MLA_HEADS * MLA_V), lambda i: (i, 0))],
        out_shape=[jax.ShapeDtypeStruct((m, MLA_HEADS * MLA_PAD), BF16),
                   jax.ShapeDtypeStruct((m, MLA_HEADS * MLA_PAD), BF16),
                   jax.ShapeDtypeStruct((m, MLA_HEADS * MLA_V), BF16)],
        compiler_params=_params("parallel"),
        name="mla_prep",
    )(seg3, cosf, sinf, wuq_p, wukv_b, cq_gain.reshape(1, -1), ckv_gain.reshape(1, -1), qgn, qgr, kgn, kgr)


_FW = FOX_HEADS * FOX_DIM
_IN_SIZES = (_FW, _FW, _FW, FOX_HEADS, ML_HEADS * ML_QK, ML_HEADS * ML_QK, ML_HEADS * ML_V,
             ML_HEADS, ML_HEADS, ML_HEADS * ML_V, MLA_Q_RANK, MLA_KV_RANK, MLA_ROPE)
_IN_OFFS = [sum(_IN_SIZES[:i]) for i in range(len(_IN_SIZES) + 1)]
(_C_FQ, _, _, _C_FF, _C_MQ, _, _, _C_MI, _C_MF, _C_MO, _C_CQ, _, _, _C_GATES) = _IN_OFFS


def _small_gate_weight(w_in_t, l):
    d = w_in_t.shape[2]
    rows = [jnp.zeros((MLA_ROPE, d), w_in_t.dtype),
            w_in_t[l, _C_FF:_C_FF + FOX_HEADS],
            w_in_t[l, _C_MI:_C_MI + ML_HEADS],
            w_in_t[l, _C_MF:_C_MF + ML_HEADS],
            jnp.zeros((LANES - MLA_ROPE - FOX_HEADS - 2 * ML_HEADS, d), w_in_t.dtype)]
    return jnp.concatenate(rows, axis=0)


def _layer(x, cosf, sinf, b, s, p, big, l):
    m, d = x.shape
    h1, g_pre = _rmsnorm(x, p["mix_norm"], _small_gate_weight(big["w_in_t"], l))
    w_in = (big["w_in_t"], l)

    def in_proj(first_col, width, out_dtype, name, extras=(), epilogue=_ep_plain, tn=1024):
        return _ws_matmul([(h1, d, 0)], [(w_in, d, 0, 0)], list(extras), epilogue, tm=1024, tn=tn, n_out=width,
                          out_dtype=out_dtype, first_row=first_col, name=name)

    proj_fox = in_proj(_C_FQ, 3 * _FW, BF16, "proj_fox")
    proj_ml = in_proj(_C_MQ, 2 * ML_HEADS * ML_QK + ML_HEADS * ML_V, BF16, "proj_ml")
    proj_mo = in_proj(_C_MO, ML_HEADS * ML_V, BF16, "proj_mo")
    w_mla = MLA_Q_RANK + MLA_KV_RANK + LANES
    seg3 = in_proj(_C_CQ, w_mla, F32, "proj_mla", tn=w_mla // 2)
    gates = in_proj(_C_GATES, 3 * d, BF16, "gates", [(p["gate_bias"].reshape(1, -1), "row", 0)], _ep_bias_sigmoid)

    bias_vec = jnp.concatenate([jnp.zeros((MLA_ROPE,), F32), p["fox_f_bias"], p["mlstm_i_bias"], p["mlstm_f_bias"],
                                jnp.zeros((LANES - MLA_ROPE - 24,), F32)]).reshape(1, LANES)
    rows = _gate_rows(g_pre, bias_vec, b, s)

    proj_fox_3d = proj_fox.reshape(b, s, -1)
    o_fox = _flash(proj_fox_3d, 0, proj_fox_3d, FOX_HEADS, proj_fox_3d, 2 * FOX_HEADS, FOX_DIM, FOX_HEADS,
                   FOX_HEADS * FOX_DIM, fox=(rows, p["fox_q_gain"], p["fox_k_gain"])).reshape(m, -1)
    o_ml = _mlstm(proj_ml.reshape(b, s, -1), proj_mo.reshape(b, s, -1), rows, p["mlstm_out_gain"],
                  p["mlstm_conv_w"], p["mlstm_conv_b"]).reshape(m, -1)
    wuq = p["mla_w_uq"].reshape(MLA_Q_RANK, MLA_HEADS, MLA_QK)
    wuq_p = jnp.concatenate([
        wuq[:, :, :MLA_NOPE].reshape(MLA_Q_RANK, -1),
        jnp.pad(wuq[:, :, MLA_NOPE:], ((0, 0), (0, 0), (0, LANES - MLA_ROPE))).reshape(MLA_Q_RANK, -1)],
        axis=1).astype(BF16)
    q_mla, k_mla, v_mla = _mla_prep(seg3, cosf, sinf, wuq_p, p["mla_w_ukv"].astype(BF16), p["mla_cq_gain"],
                                    p["mla_ckv_gain"], p["mla_q_gain"], p["mla_k_gain"])
    o_mla = _flash(q_mla.reshape(b, s, -1), 0, k_mla.reshape(b, s, -1), 0, v_mla.reshape(b, s, -1), 0,
                   MLA_PAD, MLA_HEADS, MLA_HEADS * MLA_V).reshape(m, -1)

    kb = o_fox.shape[1]
    tn_merge = 1024
    nb = d // tn_merge
    merged = _ws_matmul([(o_fox, kb, 0), (o_ml, kb, 0), (o_mla, kb, 0)],
                        [((big["w_fox_out"], l), kb, 0, 0), ((big["w_mlstm_out"], l), kb, 0, 0),
                         ((big["w_mla_out"], l), kb, 0, 0)],
                        [(gates, "tile", 0), (gates, "tile", nb), (gates, "tile", 2 * nb)],
                        _ep_merge, tm=1024, tn=tn_merge, n_out=d, out_dtype=BF16, name="merge")
    x = _ws_matmul([(merged, d, 0)], [((big["w_o"], l), d, 0, 0)], [(x, "tile", 0)], _ep_residual,
                   tm=1024, tn=512, n_out=d, out_dtype=F32, name="w_o")

    h2 = _rmsnorm(x, p["ffn_norm"])
    d_ff = big["w_gate"].shape[2]
    act = _ws_matmul([(h2, d, 0)], [((big["w_gate"], l), d, 0, 0), ((big["w_up"], l), d, 0, 0)], [], _ep_swiglu,
                     tm=2048, tn=256, n_out=d_ff, out_dtype=BF16, pairs=[(0, 0), (0, 1)], name="gate_up")
    k_first = pl.cdiv(d_ff // MXU_DEPTH, 2) * MXU_DEPTH
    for ki, (k0, kb) in enumerate(((0, k_first), (k_first, d_ff - k_first))):
        x = _ws_matmul([(act, kb, k0)], [((big["w_down"], l), kb, k0, 0)], [(x, "tile", 0)], _ep_residual,
                       tm=1024, tn=512, n_out=d, out_dtype=F32, name=f"down{ki}")
    return x


def kernel(x, positions, mix_norm, w_in, fox_f_bias, fox_q_gain, fox_k_gain, mlstm_conv_w, mlstm_conv_b,
           mlstm_i_bias, mlstm_f_bias, mlstm_out_gain, mla_cq_gain, mla_ckv_gain, mla_w_uq, mla_w_ukv,
           mla_q_gain, mla_k_gain, w_fox_out, w_mlstm_out, w_mla_out, gate_bias, w_o, ffn_norm, w_gate,
           w_up, w_down):
    b, s, d = x.shape
    depth = w_in.shape[0]
    inv_freq = jnp.power(ROPE_THETA, -jnp.arange(0, MLA_ROPE, 2, dtype=F32) / MLA_ROPE)
    ang = positions.astype(F32).reshape(b * s, 1) * inv_freq
    zpad = jnp.zeros((b * s, LANES - MLA_ROPE), F32)
    cosf = jnp.concatenate([jnp.cos(ang), jnp.cos(ang), zpad], axis=1)
    sinf = jnp.concatenate([-jnp.sin(ang), jnp.sin(ang), zpad], axis=1)
    names = dict(mix_norm=mix_norm, fox_f_bias=fox_f_bias, fox_q_gain=fox_q_gain, fox_k_gain=fox_k_gain,
                 mlstm_conv_w=mlstm_conv_w, mlstm_conv_b=mlstm_conv_b, mlstm_i_bias=mlstm_i_bias,
                 mlstm_f_bias=mlstm_f_bias, mlstm_out_gain=mlstm_out_gain, mla_cq_gain=mla_cq_gain,
                 mla_ckv_gain=mla_ckv_gain, mla_w_uq=mla_w_uq, mla_w_ukv=mla_w_ukv, mla_q_gain=mla_q_gain,
                 mla_k_gain=mla_k_gain, gate_bias=gate_bias, ffn_norm=ffn_norm)
    big = dict(w_in_t=jnp.swapaxes(w_in, 1, 2), w_fox_out=w_fox_out, w_mlstm_out=w_mlstm_out, w_mla_out=w_mla_out, w_o=w_o, w_gate=w_gate,
               w_up=w_up, w_down=w_down)
    xf = x.reshape(b * s, d)
    for l in range(depth):
        xf = _layer(xf, cosf, sinf, b, s, {k: v[l] for k, v in names.items()}, big, l)
    return xf.reshape(b, s, d)
```

```python
import functools

import jax
import jax.numpy as jnp
from jax import lax
from jax.experimental import pallas as pl
from jax.experimental.pallas import tpu as pltpu

F32 = jnp.float32
BF16 = jnp.bfloat16

NORM_EPS = 1e-6
ROPE_THETA = 10000.0
LOG2E = 1.4426950408889634

FOX_HEADS = 8
FOX_DIM = 128
ML_HEADS = 8
ML_QK = 64
ML_V = 128
ML_CONV = 4
MLA_HEADS = 8
MLA_Q_RANK = 896
MLA_KV_RANK = 512
MLA_NOPE = 128
MLA_ROPE = 64
MLA_V = 128
MLA_QK = MLA_NOPE + MLA_ROPE
MLA_PAD = 256

LANES = 128
SUBLANES = 8
MXU_DEPTH = 256
VMEM_LIMIT = 56 * 1024 * 1024
ML_CHUNK = 512
ATT_BLOCK = 256
PAD = 8


def _params(*sem):
    return pltpu.CompilerParams(dimension_semantics=sem, vmem_limit_bytes=VMEM_LIMIT)


def _rmsnorm_kernel(x_ref, g_ref, *rest):
    x = x_ref[...]
    ms = jnp.mean(x * x, axis=-1, keepdims=True)
    h = (x * lax.rsqrt(ms + NORM_EPS) * g_ref[...]).astype(BF16)
    if len(rest) == 1:
        rest[0][...] = h
    else:
        w_ref, o_ref, gp_ref = rest
        o_ref[...] = h
        gp_ref[...] = lax.dot_general(h, w_ref[...].astype(BF16), (((1,), (1,)), ((), ())),
                                      preferred_element_type=F32)


def _rmsnorm(x, gain, w_small_t=None, tm=512):
    m, d = x.shape
    in_specs = [pl.BlockSpec((tm, d), lambda i: (i, 0)),
                pl.BlockSpec((1, d), lambda i: (0, 0))]
    out_specs = pl.BlockSpec((tm, d), lambda i: (i, 0))
    out_shape = jax.ShapeDtypeStruct((m, d), BF16)
    args = [x, gain.reshape(1, d)]
    if w_small_t is not None:
        in_specs.append(pl.BlockSpec(w_small_t.shape, lambda i: (0, 0)))
        out_specs = [out_specs, pl.BlockSpec((tm, LANES), lambda i: (i, 0))]
        out_shape = [out_shape, jax.ShapeDtypeStruct((m, LANES), F32)]
        args.append(w_small_t)
    return pl.pallas_call(
        _rmsnorm_kernel,
        grid=(m // tm,),
        in_specs=in_specs,
        out_specs=out_specs,
        out_shape=out_shape,
        compiler_params=_params("parallel"),
        name="rmsnorm",
    )(*args)


def _ws_kernel(*refs, nx, nw, ne, pairs, epilogue, cast_rows, wt, panel_src):
    x_refs = refs[:nx]
    w_hbm = refs[nx:nx + nw]
    e_refs = refs[nx + nw:nx + nw + ne]
    o_ref = refs[nx + nw + ne]
    scratch = refs[nx + nw + ne + 1:]
    stage, w_bf, sem = scratch[:nw], scratch[nw:2 * nw], scratch[2 * nw]
    n = pl.program_id(0)
    n_panels = pl.num_programs(0)

    def panel_copy(wi, panel):
        return pltpu.make_async_copy(panel_src[wi](w_hbm[wi], panel), stage[wi], sem.at[wi])

    @pl.when(pl.program_id(1) == 0)
    def _():
        @pl.when(n == 0)
        def _():
            for wi in range(nw):
                panel_copy(wi, 0).start()

        for wi in range(nw):
            panel_copy(wi, n).wait()

            def body(i, c, wi=wi):
                r = pl.multiple_of(i * cast_rows, cast_rows)
                w_bf[wi][pl.ds(r, cast_rows), :] = stage[wi][pl.ds(r, cast_rows), :].astype(BF16)
                return c

            lax.fori_loop(0, stage[wi].shape[0] // cast_rows, body, 0)

        @pl.when(n + 1 < n_panels)
        def _():
            for wi in range(nw):
                panel_copy(wi, n + 1).start()

    dims = (((1,), (1 if wt else 0,)), ((), ()))
    dots = [lax.dot_general(x_refs[xi][...], w_bf[wi][...], dims, preferred_element_type=F32) for xi, wi in pairs]
    o_ref[...] = epilogue(dots, [e[...] for e in e_refs]).astype(o_ref.dtype)


def _ws_matmul(xs, ws, extras, epilogue, *, tm, tn, n_out, out_dtype, pairs=None, wt=False, first_row=None, name):
    m = xs[0][0].shape[0]
    assert n_out % tn == 0 and m % tm == 0
    pairs = pairs or [(i, i) for i in range(len(ws))]
    wt = wt or first_row is not None
    in_specs = []
    args = []
    panel_src = []
    for a, kb, k0 in xs:
        if k0 % kb == 0:
            in_specs.append(pl.BlockSpec((tm, kb), lambda n, i, kblk=k0 // kb: (i, kblk)))
        else:
            assert k0 % LANES == 0 and k0 + kb <= a.shape[1]
            in_specs.append(pl.BlockSpec((pl.Element(tm), pl.Element(kb)), lambda n, i, k0=k0: (i * tm, k0)))
        args.append(a)
    for a, kb, k0, off in ws:
        layer = None
        if isinstance(a, tuple):
            a, layer = a
        assert a.dtype == F32
        if first_row is not None:
            assert len(ws) == 1 and kb == a.shape[2] and first_row % SUBLANES == 0 and tn % SUBLANES == 0
            assert first_row + n_out <= a.shape[1]
            panel_src.append(lambda w, p, layer=layer: w.at[
                layer, pl.ds(pl.multiple_of(first_row + p * tn, SUBLANES), tn), :])
        elif wt:
            assert layer is None and kb == a.shape[1]
            panel_src.append(lambda w, p: w.at[pl.ds(pl.multiple_of(p * tn, SUBLANES), tn), :])
        else:
            assert layer is not None
            panel_src.append(lambda w, p, layer=layer, kb=kb, k0=k0, off=off: w.at[
                layer, pl.ds(k0, kb), pl.ds(pl.multiple_of((p + off) * tn, LANES), tn)])
        in_specs.append(pl.BlockSpec(memory_space=pl.ANY))
        args.append(a)
    for a, kind, off in extras:
        if kind == "tile":
            in_specs.append(pl.BlockSpec((tm, tn), lambda n, i, off=off: (i, n + off)))
        else:
            in_specs.append(pl.BlockSpec((1, tn), lambda n, i, off=off: (0, n + off)))
        args.append(a)
    panel_shapes = [(tn, kb) if wt else (kb, tn) for _, kb, _, _ in ws]
    cast_rows = LANES
    for shp in panel_shapes:
        assert shp[0] % cast_rows == 0, shp
    scratch = ([pltpu.VMEM(shp, F32) for shp in panel_shapes] + [pltpu.VMEM(shp, BF16) for shp in panel_shapes]
               + [pltpu.SemaphoreType.DMA((len(ws),))])
    kern = functools.partial(_ws_kernel, nx=len(xs), nw=len(ws), ne=len(extras), pairs=pairs, epilogue=epilogue,
                             cast_rows=cast_rows, wt=wt, panel_src=panel_src)
    return pl.pallas_call(
        kern,
        grid=(n_out // tn, m // tm),
        in_specs=in_specs,
        out_specs=pl.BlockSpec((tm, tn), lambda n, i: (i, n)),
        out_shape=jax.ShapeDtypeStruct((m, n_out), out_dtype),
        scratch_shapes=scratch,
        compiler_params=_params("arbitrary", "arbitrary"),
        name=name,
    )(*args)


def _ep_plain(dots, extras):
    return dots[0]


def _sigmoid(x):
    return 0.5 * jnp.tanh(0.5 * x) + 0.5


def _ep_bias_sigmoid(dots, extras):
    return _sigmoid(dots[0] + extras[0])


def _ep_residual(dots, extras):
    return extras[0] + dots[0]


def _ep_swiglu(dots, extras):
    g, u = dots
    return g * _sigmoid(g) * u


def _ep_merge(dots, extras):
    return (extras[0].astype(F32) * dots[0] + extras[1].astype(F32) * dots[1]
            + extras[2].astype(F32) * dots[2])


def _log_sigmoid(x):
    return jnp.minimum(x, 0.0) - jnp.log1p(jnp.exp(-jnp.abs(x)))


def _cumsum_lanes(x, seg):
    n = x.shape[-1]
    pos = lax.broadcasted_iota(jnp.int32, x.shape, 1) % seg
    shift = 1
    while shift < seg:
        x = x + jnp.where(pos >= shift, pltpu.roll(x, shift, axis=1), 0.0)
        shift *= 2
    del n
    return x


def _gates_kernel(g_ref, bias_ref, rows_ref, *, chunk):
    g = g_ref[...] + bias_ref[...]
    gt = g.T
    s = gt.shape[1]
    rows_ref[0:8, :] = _cumsum_lanes(_log_sigmoid(gt[64:72, :]), s) * LOG2E
    rows_ref[8:16, :] = gt[72:80, :] * LOG2E
    rows_ref[16:24, :] = _cumsum_lanes(_log_sigmoid(gt[80:88, :]), chunk) * LOG2E


def _gate_rows(g_pre, bias_vec, b, s):
    return pl.pallas_call(
        functools.partial(_gates_kernel, chunk=ML_CHUNK),
        grid=(b,),
        in_specs=[pl.BlockSpec((s, LANES), lambda i: (i, 0)),
                  pl.BlockSpec((1, LANES), lambda i: (0, 0))],
        out_specs=pl.BlockSpec((None, 24, s), lambda i: (i, 0, 0)),
        out_shape=jax.ShapeDtypeStruct((b, 24, s), F32),
        compiler_params=_params("parallel"),
        name="gate_rows",
    )(g_pre, bias_vec)


def _flash_kernel(*refs, blk, has_bias):
    if has_bias:
        q_ref, kraw_ref, v_ref, d_ref, qg_ref, kg_ref, o_ref, k_ref = refs
    else:
        q_ref, k_ref, v_ref, o_ref = refs
        d_ref = None
    h = pl.program_id(1)
    s_len = k_ref.shape[0]
    row = lax.broadcasted_iota(jnp.int32, (blk, blk), 0)
    col = lax.broadcasted_iota(jnp.int32, (blk, blk), 1)
    nt = (((1,), (1,)), ((), ()))

    def qk_norm(x, gain):
        xf = x.astype(F32)
        return xf * lax.rsqrt(jnp.mean(xf * xf, axis=-1, keepdims=True) + NORM_EPS) * gain

    if has_bias:
        for c in range(s_len // blk):
            k_ref[c * blk:(c + 1) * blk, :] = qk_norm(kraw_ref[c * blk:(c + 1) * blk, :], kg_ref[...]).astype(BF16)
    for qi in range(s_len // blk):
        lo = qi * blk
        q = q_ref[lo:lo + blk, :]
        if has_bias:
            q = (qk_norm(q, qg_ref[...]) * (FOX_DIM ** -0.5 * LOG2E)).astype(BF16)
        s_diag = lax.dot_general(q, k_ref[lo:lo + blk, :], nt, preferred_element_type=F32)
        if has_bias:
            s_diag = s_diag - d_ref[pl.ds(h, 1), lo:lo + blk]
        s_diag = jnp.where(col <= row, s_diag, -jnp.inf)
        m = jnp.max(s_diag, axis=-1, keepdims=True)
        if qi:
            s_off = lax.dot_general(q, k_ref[0:lo, :], nt, preferred_element_type=F32)
            if has_bias:
                s_off = s_off - d_ref[pl.ds(h, 1), 0:lo]
            m = jnp.maximum(m, jnp.max(s_off, axis=-1, keepdims=True))
        p_diag = jnp.exp2(s_diag - m)
        l = jnp.sum(p_diag, axis=-1, keepdims=True)
        acc = jnp.dot(p_diag.astype(BF16), v_ref[lo:lo + blk, :], preferred_element_type=F32)
        if qi:
            p_off = jnp.exp2(s_off - m)
            l = l + jnp.sum(p_off, axis=-1, keepdims=True)
            acc = acc + jnp.dot(p_off.astype(BF16), v_ref[0:lo, :], preferred_element_type=F32)
        o_ref[lo:lo + blk, :] = (acc * (1.0 / l)).astype(o_ref.dtype)


def _flash(q_arr, q_off, k_arr, k_off, v_arr, v_off, dk, n_heads, out_width, fox=None):
    b, s, _ = q_arr.shape
    blk = ATT_BLOCK
    in_specs = [
        pl.BlockSpec((None, s, dk), lambda bi, h: (bi, 0, q_off + h)),
        pl.BlockSpec((None, s, dk), lambda bi, h: (bi, 0, k_off + h)),
        pl.BlockSpec((None, s, LANES), lambda bi, h: (bi, 0, v_off + h)),
    ]
    args = [q_arr, k_arr, v_arr]
    scratch = []
    if fox is not None:
        rows, q_gain, k_gain = fox
        in_specs += [pl.BlockSpec((None, 8, s), lambda bi, h: (bi, 0, 0)),
                     pl.BlockSpec((1, dk), lambda bi, h: (0, 0)),
                     pl.BlockSpec((1, dk), lambda bi, h: (0, 0))]
        args += [rows, q_gain.reshape(1, dk), k_gain.reshape(1, dk)]
        scratch = [pltpu.VMEM((s, dk), BF16)]
    return pl.pallas_call(
        functools.partial(_flash_kernel, blk=blk, has_bias=fox is not None),
        grid=(b, n_heads),
        in_specs=in_specs,
        out_specs=pl.BlockSpec((None, s, LANES), lambda bi, h: (bi, 0, h)),
        out_shape=jax.ShapeDtypeStruct((b, s, out_width), BF16),
        scratch_shapes=scratch,
        compiler_params=_params("parallel", "parallel"),
        name="flash_bias" if fox is not None else "flash",
    )(*args)


def _mlstm_kernel(q_ref, k_ref, v_ref, og_ref, rows_ref, gain_ref, wq_ref, wk_ref, bq_ref, bk_ref, o_ref,
                  xq_ref, xk_ref, *, chunk):
    L = chunk
    hp = pl.program_id(1)
    s = q_ref.shape[0]
    nc = s // L
    for x_ref, xs_ref in ((q_ref, xq_ref), (k_ref, xk_ref)):
        xs_ref[0:PAD, :] = jnp.zeros((PAD, LANES), F32)
        for c in range(nc):
            xs_ref[PAD + c * L:PAD + (c + 1) * L, :] = x_ref[c * L:(c + 1) * L, :].astype(F32)

    def conv_silu(xs_ref, w_ref, b_ref, t0):
        y = b_ref[...]
        for j in range(ML_CONV):
            off = PAD + t0 - (ML_CONV - 1) + j
            y = y + w_ref[j:j + 1, :] * xs_ref[off:off + L, :]
        return y * _sigmoid(y)

    lane = lax.broadcasted_iota(jnp.int32, (1, LANES), 1)
    row = lax.broadcasted_iota(jnp.int32, (L, L), 0)
    col = lax.broadcasted_iota(jnp.int32, (L, L), 1)
    causal = col <= row
    eye = col == row
    col1 = lax.broadcasted_iota(jnp.int32, (1, L), 1)
    row1 = lax.broadcasted_iota(jnp.int32, (L, 1), 0)

    def to_col(r):
        return jnp.sum(jnp.where(eye, r, 0.0), axis=-1, keepdims=True)

    def head_step(t0, hh, qp, kp, kp_b, state):
        c_state, n_state, m_prev = state
        head = 2 * hp + hh
        li = rows_ref[pl.ds(8 + head, 1), pl.ds(t0, L)]
        bc = rows_ref[pl.ds(16 + head, 1), pl.ds(t0, L)]
        a_row = li - bc
        bc_col = to_col(bc)
        log_w = jnp.where(causal, bc_col + a_row, -jnp.inf)
        log_inter = bc_col + m_prev
        m_t = jnp.maximum(log_inter, jnp.max(log_w, axis=-1, keepdims=True))
        w = jnp.exp2(log_w - m_t)
        inter = jnp.exp2(log_inter - m_t)
        head_lanes = (lane >= hh * ML_QK) & (lane < (hh + 1) * ML_QK)
        qh = jnp.where(head_lanes, qp, jnp.zeros_like(qp))
        qk = lax.dot_general(qh, kp_b, (((1,), (1,)), ((), ())), preferred_element_type=F32) * w
        vh = v_ref[pl.ds(t0, L), hh * ML_V:(hh + 1) * ML_V]
        num = (jnp.dot(qk.astype(BF16), vh, preferred_element_type=F32)
               + inter * jnp.dot(qh, c_state.astype(BF16), preferred_element_type=F32))
        den = (jnp.sum(qk, axis=-1, keepdims=True)
               + inter * jnp.sum(qh.astype(F32) * n_state, axis=-1, keepdims=True))
        h_out = num * (1.0 / jnp.maximum(jnp.abs(den), jnp.exp2(-m_t)))
        hn = h_out * lax.rsqrt(jnp.mean(h_out * h_out, axis=-1, keepdims=True) + NORM_EPS)
        hn = hn * gain_ref[:, hh * ML_V:(hh + 1) * ML_V]
        og = og_ref[pl.ds(t0, L), hh * ML_V:(hh + 1) * ML_V].astype(F32)
        o_ref[pl.ds(t0, L), hh * ML_V:(hh + 1) * ML_V] = (hn * _sigmoid(og)).astype(o_ref.dtype)
        m_new = jnp.sum(jnp.where(row1 == L - 1, m_t, 0.0), axis=0, keepdims=True)
        b_last = jnp.sum(jnp.where(col1 == L - 1, bc, 0.0), axis=-1, keepdims=True)
        decay = jnp.exp2(b_last + m_prev - m_new)
        w_last = to_col(jnp.exp2(b_last + a_row - m_new))
        kw = kp * w_last
        c_upd = lax.dot_general(kw.astype(BF16), vh, (((0,), (0,)), ((), ())), preferred_element_type=F32)
        return decay * c_state + c_upd, decay * n_state + jnp.sum(kw, axis=0, keepdims=True), m_new

    states = [(jnp.zeros((LANES, ML_V), F32), jnp.zeros((1, LANES), F32), jnp.zeros((1, 1), F32))
              for _ in range(2)]
    for ci in range(nc):
        t0 = ci * L
        qp = conv_silu(xq_ref, wq_ref, bq_ref, t0).astype(BF16)
        kp = conv_silu(xk_ref, wk_ref, bk_ref, t0) * (ML_QK ** -0.5)
        kp_b = kp.astype(BF16)
        states = [head_step(t0, hh, qp, kp, kp_b, states[hh]) for hh in range(2)]


def _mlstm(qkv_arr, og_arr, rows, out_gain, conv_w, conv_b):
    b, s, _ = qkv_arr.shape
    npair = ML_HEADS // 2
    v_blk0 = 2 * ML_HEADS * ML_QK // (2 * ML_V)
    pair = lambda off: (lambda bi, hp: (0, off + hp))
    return pl.pallas_call(
        functools.partial(_mlstm_kernel, chunk=ML_CHUNK),
        grid=(b, npair),
        in_specs=[pl.BlockSpec((None, s, LANES), lambda bi, hp: (bi, 0, hp)),
                  pl.BlockSpec((None, s, LANES), lambda bi, hp: (bi, 0, npair + hp)),
                  pl.BlockSpec((None, s, 2 * ML_V), lambda bi, hp: (bi, 0, v_blk0 + hp)),
                  pl.BlockSpec((None, s, 2 * ML_V), lambda bi, hp: (bi, 0, hp)),
                  pl.BlockSpec((None, 24, s), lambda bi, hp: (bi, 0, 0)),
                  pl.BlockSpec((1, 2 * ML_V), pair(0)),
                  pl.BlockSpec((ML_CONV, LANES), pair(0)),
                  pl.BlockSpec((ML_CONV, LANES), pair(npair)),
                  pl.BlockSpec((1, LANES), pair(0)),
                  pl.BlockSpec((1, LANES), pair(npair))],
        out_specs=pl.BlockSpec((None, s, 2 * ML_V), lambda bi, hp: (bi, 0, hp)),
        out_shape=jax.ShapeDtypeStruct((b, s, ML_HEADS * ML_V), BF16),
        scratch_shapes=[pltpu.VMEM((s + PAD, LANES), F32), pltpu.VMEM((s + PAD, LANES), F32)],
        compiler_params=_params("parallel", "parallel"),
        name="mlstm",
    )(qkv_arr, qkv_arr, qkv_arr, og_arr, rows, out_gain.reshape(1, -1), conv_w, conv_w,
      conv_b.reshape(1, -1), conv_b.reshape(1, -1))


def _mla_prep_kernel(s_ref, cos_ref, sin_ref, wuq_ref, wukv_ref, cqg_ref, ckvg_ref, qgn_ref, qgr_ref,
                     kgn_ref, kgr_ref, q_ref, k_ref, v_ref):
    scale = MLA_QK ** -0.5 * LOG2E
    lane = lax.broadcasted_iota(jnp.int32, (1, LANES), 1)
    cosf = cos_ref[...]
    sinf = sin_ref[...]

    def norm(x, g):
        return x * lax.rsqrt(jnp.mean(x * x, axis=-1, keepdims=True) + NORM_EPS) * g

    def rope(x):
        swap = jnp.where(lane < MLA_ROPE // 2, pltpu.roll(x, LANES - MLA_ROPE // 2, axis=1),
                         pltpu.roll(x, MLA_ROPE // 2, axis=1))
        return x * cosf + swap * sinf

    cq = s_ref[:, 0:MLA_Q_RANK]
    ckv = s_ref[:, MLA_Q_RANK:MLA_Q_RANK + MLA_KV_RANK]
    kr = jnp.where(lane < MLA_ROPE, s_ref[:, MLA_Q_RANK + MLA_KV_RANK:], 0.0)
    qf = jnp.dot(norm(cq, cqg_ref[...]).astype(BF16), wuq_ref[...], preferred_element_type=F32)
    kvf = jnp.dot(norm(ckv, ckvg_ref[...]).astype(BF16), wukv_ref[...], preferred_element_type=F32)
    ss_kr = jnp.sum(kr * kr, axis=-1, keepdims=True)
    kr_rot = rope(kr * kgr_ref[...])
    qgn_s = qgn_ref[...] * scale
    qgr_s = qgr_ref[...] * scale
    nw = MLA_HEADS * MLA_NOPE
    for h in range(MLA_HEADS):
        qn = qf[:, h * MLA_NOPE:(h + 1) * MLA_NOPE]
        qr = qf[:, nw + h * LANES:nw + (h + 1) * LANES]
        ss = jnp.sum(qn * qn, axis=-1, keepdims=True) + jnp.sum(qr * qr, axis=-1, keepdims=True)
        r = lax.rsqrt(ss / MLA_QK + NORM_EPS)
        q_ref[:, h * MLA_PAD:h * MLA_PAD + MLA_NOPE] = ((qn * r) * qgn_s).astype(q_ref.dtype)
        q_ref[:, h * MLA_PAD + MLA_NOPE:(h + 1) * MLA_PAD] = rope((qr * r) * qgr_s).astype(q_ref.dtype)
        kn = kvf[:, h * 2 * MLA_NOPE:h * 2 * MLA_NOPE + MLA_NOPE]
        rk = lax.rsqrt((jnp.sum(kn * kn, axis=-1, keepdims=True) + ss_kr) / MLA_QK + NORM_EPS)
        k_ref[:, h * MLA_PAD:h * MLA_PAD + MLA_NOPE] = ((kn * rk) * kgn_ref[...]).astype(k_ref.dtype)
        k_ref[:, h * MLA_PAD + MLA_NOPE:(h + 1) * MLA_PAD] = (kr_rot * rk).astype(k_ref.dtype)
        v_ref[:, h * MLA_V:(h + 1) * MLA_V] = kvf[:, h * 2 * MLA_NOPE + MLA_NOPE:(h + 1) * 2 * MLA_NOPE].astype(v_ref.dtype)


def _mla_prep(seg3, cosf, sinf, wuq_p, wukv_b, cq_gain, ckv_gain, q_gain, k_gain, tm=256):
    m, w3 = seg3.shape
    pad = jnp.zeros((LANES - MLA_ROPE,), F32)
    qgn = q_gain[:MLA_NOPE].reshape(1, -1)
    qgr = jnp.concatenate([q_gain[MLA_NOPE:], pad]).reshape(1, -1)
    kgn = k_gain[:MLA_NOPE].reshape(1, -1)
    kgr = jnp.concatenate([k_gain[MLA_NOPE:], pad]).reshape(1, -1)
    const = lambda i: (0, 0)
    return pl.pallas_call(
        _mla_prep_kernel,
        grid=(m // tm,),
        in_specs=[pl.BlockSpec((tm, w3), lambda i: (i, 0)),
                  pl.BlockSpec((tm, LANES), lambda i: (i, 0)),
                  pl.BlockSpec((tm, LANES), lambda i: (i, 0)),
                  pl.BlockSpec(wuq_p.shape, const),
                  pl.BlockSpec(wukv_b.shape, const),
                  pl.BlockSpec((1, MLA_Q_RANK), const),
                  pl.BlockSpec((1, MLA_KV_RANK), const),
                  pl.BlockSpec((1, LANES), const),
                  pl.BlockSpec((1, LANES), const),
                  pl.BlockSpec((1, LANES), const),
                  pl.BlockSpec((1, LANES), const)],
        out_specs=[pl.BlockSpec((tm, MLA_HEADS * MLA_PAD), lambda i: (i, 0)),
                   pl.BlockSpec((tm, MLA_HEADS * MLA_PAD), lambda i: (i, 0)),
                   pl.BlockSpec((tm, MLA_HEADS * MLA_V), lambda i: (i, 0))],
        out_shape=[jax.ShapeDtypeStruct((m, MLA_HEADS * MLA_PAD), BF16),
                   jax.ShapeDtypeStruct((m, MLA_HEADS * MLA_PAD), BF16),
                   jax.ShapeDtypeStruct((m, MLA_HEADS * MLA_V), BF16)],
        compiler_params=_params("parallel"),
        name="mla_prep",
    )(seg3, cosf, sinf, wuq_p, wukv_b, cq_gain.reshape(1, -1), ckv_gain.reshape(1, -1), qgn, qgr, kgn, kgr)


_FW = FOX_HEADS * FOX_DIM
_IN_SIZES = (_FW, _FW, _FW, FOX_HEADS, ML_HEADS * ML_QK, ML_HEADS * ML_QK, ML_HEADS * ML_V,
             ML_HEADS, ML_HEADS, ML_HEADS * ML_V, MLA_Q_RANK, MLA_KV_RANK, MLA_ROPE)
_IN_OFFS = [sum(_IN_SIZES[:i]) for i in range(len(_IN_SIZES) + 1)]
(_C_FQ, _, _, _C_FF, _C_MQ, _, _, _C_MI, _C_MF, _C_MO, _C_CQ, _, _, _C_GATES) = _IN_OFFS


def _small_gate_weight(w_in_t, l):
    d = w_in_t.shape[2]
    rows = [jnp.zeros((MLA_ROPE, d), w_in_t.dtype),
            w_in_t[l, _C_FF:_C_FF + FOX_HEADS],
            w_in_t[l, _C_MI:_C_MI + ML_HEADS],
            w_in_t[l, _C_MF:_C_MF + ML_HEADS],
            jnp.zeros((LANES - MLA_ROPE - FOX_HEADS - 2 * ML_HEADS, d), w_in_t.dtype)]
    return jnp.concatenate(rows, axis=0)


def _layer(x, cosf, sinf, b, s, p, big, l):
    m, d = x.shape
    h1, g_pre = _rmsnorm(x, p["mix_norm"], _small_gate_weight(big["w_in_t"], l))
    w_in = (big["w_in_t"], l)

    def in_proj(first_col, width, out_dtype, name, extras=(), epilogue=_ep_plain, tn=1024):
        return _ws_matmul([(h1, d, 0)], [(w_in, d, 0, 0)], list(extras), epilogue, tm=1024, tn=tn, n_out=width,
                          out_dtype=out_dtype, first_row=first_col, name=name)

    proj_fox = in_proj(_C_FQ, 3 * _FW, BF16, "proj_fox")
    proj_ml = in_proj(_C_MQ, 2 * ML_HEADS * ML_QK + ML_HEADS * ML_V, BF16, "proj_ml")
    proj_mo = in_proj(_C_MO, ML_HEADS * ML_V, BF16, "proj_mo")
    w_mla = MLA_Q_RANK + MLA_KV_RANK + LANES
    seg3 = in_proj(_C_CQ, w_mla, F32, "proj_mla", tn=w_mla // 2)
    gates = in_proj(_C_GATES, 3 * d, BF16, "gates", [(p["gate_bias"].reshape(1, -1), "row", 0)], _ep_bias_sigmoid)

    bias_vec = jnp.concatenate([jnp.zeros((MLA_ROPE,), F32), p["fox_f_bias"], p["mlstm_i_bias"], p["mlstm_f_bias"],
                                jnp.zeros((LANES - MLA_ROPE - 24,), F32)]).reshape(1, LANES)
    rows = _gate_rows(g_pre, bias_vec, b, s)

    proj_fox_3d = proj_fox.reshape(b, s, -1)
    o_fox = _flash(proj_fox_3d, 0, proj_fox_3d, FOX_HEADS, proj_fox_3d, 2 * FOX_HEADS, FOX_DIM, FOX_HEADS,
                   FOX_HEADS * FOX_DIM, fox=(rows, p["fox_q_gain"], p["fox_k_gain"])).reshape(m, -1)
    o_ml = _mlstm(proj_ml.reshape(b, s, -1), proj_mo.reshape(b, s, -1), rows, p["mlstm_out_gain"],
                  p["mlstm_conv_w"], p["mlstm_conv_b"]).reshape(m, -1)
    wuq = p["mla_w_uq"].reshape(MLA_Q_RANK, MLA_HEADS, MLA_QK)
    wuq_p = jnp.concatenate([
        wuq[:, :, :MLA_NOPE].reshape(MLA_Q_RANK, -1),
        jnp.pad(wuq[:, :, MLA_NOPE:], ((0, 0), (0, 0), (0, LANES - MLA_ROPE))).reshape(MLA_Q_RANK, -1)],
        axis=1).astype(BF16)
    q_mla, k_mla, v_mla = _mla_prep(seg3, cosf, sinf, wuq_p, p["mla_w_ukv"].astype(BF16), p["mla_cq_gain"],
                                    p["mla_ckv_gain"], p["mla_q_gain"], p["mla_k_gain"])
    o_mla = _flash(q_mla.reshape(b, s, -1), 0, k_mla.reshape(b, s, -1), 0, v_mla.reshape(b, s, -1), 0,
                   MLA_PAD, MLA_HEADS, MLA_HEADS * MLA_V).reshape(m, -1)

    kb = o_fox.shape[1]
    tn_merge = 1024
    nb = d // tn_merge
    merged = _ws_matmul([(o_fox, kb, 0), (o_ml, kb, 0), (o_mla, kb, 0)],
                        [((big["w_fox_out"], l), kb, 0, 0), ((big["w_mlstm_out"], l), kb, 0, 0),
                         ((big["w_mla_out"], l), kb, 0, 0)],
                        [(gates, "tile", 0), (gates, "tile", nb), (gates, "tile", 2 * nb)],
                        _ep_merge, tm=1024, tn=tn_merge, n_out=d, out_dtype=BF16, name="merge")
    x = _ws_matmul([(merged, d, 0)], [((big["w_o"], l), d, 0, 0)], [(x, "tile", 0)], _ep_residual,
                   tm=1024, tn=512, n_out=d, out_dtype=F32, name="w_o")

    h2 = _rmsnorm(x, p["ffn_norm"])
    d_ff = big["w_gate"].shape[2]
    act = _ws_matmul([(h2, d, 0)], [((big["w_gate"], l), d, 0, 0), ((big["w_up"], l), d, 0, 0)], [], _ep_swiglu,
                     tm=2048, tn=256, n_out=d_ff, out_dtype=BF16, pairs=[(0, 0), (0, 1)], name="gate_up")
    k_first = pl.cdiv(d_ff // MXU_DEPTH, 2) * MXU_DEPTH
    for ki, (k0, kb) in enumerate(((0, k_first), (k_first, d_ff - k_first))):
        x = _ws_matmul([(act, kb, k0)], [((big["w_down"], l), kb, k0, 0)], [(x, "tile", 0)], _ep_residual,
                       tm=1024, tn=512, n_out=d, out_dtype=F32, name=f"down{ki}")
    return x


def kernel(x, positions, mix_norm, w_in, fox_f_bias, fox_q_gain, fox_k_gain, mlstm_conv_w, mlstm_conv_b,
           mlstm_i_bias, mlstm_f_bias, mlstm_out_gain, mla_cq_gain, mla_ckv_gain, mla_w_uq, mla_w_ukv,
           mla_q_gain, mla_k_gain, w_fox_out, w_mlstm_out, w_mla_out, gate_bias, w_o, ffn_norm, w_gate,
           w_up, w_down):
    b, s, d = x.shape
    depth = w_in.shape[0]
    inv_freq = jnp.power(ROPE_THETA, -jnp.arange(0, MLA_ROPE, 2, dtype=F32) / MLA_ROPE)
    ang = positions.astype(F32).reshape(b * s, 1) * inv_freq
    zpad = jnp.zeros((b * s, LANES - MLA_ROPE), F32)
    cosf = jnp.concatenate([jnp.cos(ang), jnp.cos(ang), zpad], axis=1)
    sinf = jnp.concatenate([-jnp.sin(ang), jnp.sin(ang), zpad], axis=1)
    names = dict(mix_norm=mix_norm, fox_f_bias=fox_f_bias, fox_q_gain=fox_q_gain, fox_k_gain=fox_k_gain,
                 mlstm_conv_w=mlstm_conv_w, mlstm_conv_b=mlstm_conv_b, mlstm_i_bias=mlstm_i_bias,
                 mlstm_f_bias=mlstm_f_bias, mlstm_out_gain=mlstm_out_gain, mla_cq_gain=mla_cq_gain,
                 mla_ckv_gain=mla_ckv_gain, mla_w_uq=mla_w_uq, mla_w_ukv=mla_w_ukv, mla_q_gain=mla_q_gain,
                 mla_k_gain=mla_k_gain, gate_bias=gate_bias, ffn_norm=ffn_norm)
    big = dict(w_in_t=jnp.swapaxes(w_in, 1, 2), w_fox_out=w_fox_out, w_mlstm_out=w_mlstm_out, w_mla_out=w_mla_out, w_o=w_o, w_gate=w_gate,
               w_up=w_up, w_down=w_down)
    xf = x.reshape(b * s, d)
    for l in range(depth):
        xf = _layer(xf, cosf, sinf, b, s, {k: v[l] for k, v in names.items()}, big, l)
    return xf.reshape(b, s, d)
```
